```python
import math, functools
import jax, jax.numpy as jnp
from jax import lax
import numpy as np

D_MODEL = 2048
BATCH = 4
SEQ = 2048
DEPTH = 1
DEC_BATCH = 128
DEC_SEQ = 8
PAST_LEN = 16384
PAGE_SIZE = 128

MLA_HEADS = 8
Q_LORA = 512
KV_LORA = 512
QK_NOPE = 128
QK_ROPE = 64
V_HEAD = 128
ROPE_THETA = 10000.0
MLA_SCALE = (QK_NOPE + QK_ROPE) ** -0.5
SB_HEADS = 8
SB_KV_HEADS = 2
SB_GROUP = SB_HEADS // SB_KV_HEADS
SB_HEAD_DIM = 128
SB_SCALE = SB_HEAD_DIM ** -0.5
IN_SIZES = (Q_LORA, KV_LORA, QK_ROPE, SB_HEADS * SB_HEAD_DIM, SB_KV_HEADS * SB_HEAD_DIM,
            SB_KV_HEADS * SB_HEAD_DIM, D_MODEL, D_MODEL)
IN_WIDTH = sum(IN_SIZES)
IN_SPLITS = tuple(np.cumsum(IN_SIZES)[:-1].tolist())
N_GROUPS = 4
EXPERTS_PER_GROUP = 8
N_EXPERTS = N_GROUPS * EXPERTS_PER_GROUP
TOP_K = 2
EXPERT_FF = 512
Q_BLOCK = 128
EPS = 1e-6

kernel_name = "hybrid_mla_stickbreak_hmoe_step"


def rmsnorm(x, gain):
    xf = x.astype(jnp.float32)
    y = xf * lax.rsqrt(jnp.mean(xf * xf, axis=-1, keepdims=True) + EPS)
    return (y * gain.astype(jnp.float32)).astype(x.dtype)


def rope(x, pos):
    half = QK_ROPE // 2
    inv = 1.0 / (ROPE_THETA ** (jnp.arange(half, dtype=jnp.float32) / half))
    ang = pos.astype(jnp.float32)[:, None] * inv[None, :]
    shape = (1, pos.shape[0]) + (1,) * (x.ndim - 3) + (half,)
    cos, sin = jnp.cos(ang).reshape(shape), jnp.sin(ang).reshape(shape)
    xf = x.astype(jnp.float32)
    x1, x2 = xf[..., :half], xf[..., half:]
    return jnp.concatenate([x1 * cos - x2 * sin, x2 * cos + x1 * sin], axis=-1).astype(x.dtype)


def sweep_query_blocks(block_fn, q_arrays, q_pos):
    T = q_pos.shape[0]
    blk = min(Q_BLOCK, T)
    n_blk = -(-T // blk)
    pad = n_blk * blk - T

    def to_blocks(a):
        a = jnp.pad(a, [(0, 0), (0, pad)] + [(0, 0)] * (a.ndim - 2))
        return jnp.moveaxis(a.reshape((a.shape[0], n_blk, blk) + a.shape[2:]), 1, 0)

    pos_blocks = jnp.pad(q_pos, (0, pad), mode="edge").reshape(n_blk, blk)
    xs = tuple(to_blocks(a) for a in q_arrays) + (pos_blocks,)
    out = lax.map(lambda args: block_fn(*args), xs)
    out = jnp.moveaxis(out, 0, 1)
    out = out.reshape((out.shape[0], n_blk * blk) + out.shape[3:])
    return out[:, :T]


def mla_attend(q_lat, q_pe, q_pos, key_segs):
    scores = jnp.concatenate([
        jnp.einsum("bthc,bsc->bhts", q_lat, ckv, preferred_element_type=jnp.float32)
        + jnp.einsum("bthr,bsr->bhts", q_pe, kpe, preferred_element_type=jnp.float32)
        for ckv, kpe in key_segs], axis=-1) * MLA_SCALE
    k_pos = jnp.arange(scores.shape[-1])
    mask = k_pos[None, :] <= q_pos[:, None]
    probs = jax.nn.softmax(jnp.where(mask, scores, -jnp.inf), axis=-1)
    cuts = np.cumsum([ckv.shape[1] for ckv, _ in key_segs])[:-1].tolist()
    probs_segs = jnp.split(probs, cuts, axis=-1)
    return sum(jnp.einsum("bhts,bsc->bthc", p.astype(ckv.dtype), ckv)
               for p, (ckv, _) in zip(probs_segs, key_segs))


def sb_attend(q, q_pos, key_segs):
    z = jnp.concatenate([
        jnp.einsum("btkgd,bskd->bkgts", q, k, preferred_element_type=jnp.float32)
        for k, _ in key_segs], axis=-1) * SB_SCALE
    k_pos = jnp.arange(z.shape[-1])
    mask = k_pos[None, :] < q_pos[:, None]
    log_keep = jnp.where(mask, jax.nn.log_sigmoid(-z), 0.0)
    log_survive = lax.cumsum(log_keep, axis=z.ndim - 1, reverse=True) - log_keep
    weights = jnp.where(mask, jnp.exp(jax.nn.log_sigmoid(z) + log_survive), 0.0)
    cuts = np.cumsum([k.shape[1] for k, _ in key_segs])[:-1].tolist()
    w_segs = jnp.split(weights, cuts, axis=-1)
    return sum(jnp.einsum("bkgts,bskd->btkgd", w.astype(v.dtype), v)
               for w, (_, v) in zip(w_segs, key_segs))


def token_mixer(h, offset, past, w_in, q_norm, w_uq, kv_norm, w_uk, w_uv, w_pa, w_pb, w_o):
    B, T, _ = h.shape
    q_pos = offset + jnp.arange(T)
    proj = h @ w_in
    c_q, c_kv, k_pe, sb_q, sb_k, sb_v, gate_a, gate_b = jnp.split(proj, IN_SPLITS, axis=-1)

    q = jnp.einsum("btc,chn->bthn", rmsnorm(c_q, q_norm), w_uq)
    q_nope, q_pe = q[..., :QK_NOPE], rope(q[..., QK_NOPE:], q_pos)
    c_kv = rmsnorm(c_kv, kv_norm)
    k_pe = rope(k_pe, q_pos)
    q_lat = jnp.einsum("bthn,chn->bthc", q_nope, w_uk)
    mla_segs = [(c_kv, k_pe)] if past is None else [(past[0], past[1]), (c_kv, k_pe)]
    o_lat = sweep_query_blocks(lambda a, b, p: mla_attend(a, b, p, mla_segs), (q_lat, q_pe), q_pos)
    o_a = jnp.einsum("bthc,chv->bthv", o_lat, w_uv).reshape(B, T, MLA_HEADS * V_HEAD)

    sb_q = sb_q.reshape(B, T, SB_KV_HEADS, SB_GROUP, SB_HEAD_DIM)
    sb_k = sb_k.reshape(B, T, SB_KV_HEADS, SB_HEAD_DIM)
    sb_v = sb_v.reshape(B, T, SB_KV_HEADS, SB_HEAD_DIM)
    sb_segs = [(sb_k, sb_v)] if past is None else [(past[2], past[3]), (sb_k, sb_v)]
    o_b = sweep_query_blocks(lambda a, p: sb_attend(a, p, sb_segs), (sb_q,), q_pos)
    o_b = o_b.reshape(B, T, SB_HEADS * SB_HEAD_DIM)

    merged = jax.nn.sigmoid(gate_a) * (o_a @ w_pa) + jax.nn.sigmoid(gate_b) * (o_b @ w_pb)
    return merged @ w_o, (c_kv, k_pe, sb_k, sb_v)


def hier_moe(h, w_rg, b_rg, w_re, b_re, w_gate, w_up, w_down):
    B, T, D = h.shape
    hf = h.reshape(B * T, D)
    g_logits = (hf @ w_rg).astype(jnp.float32) + b_rg.astype(jnp.float32)
    g_prob = jax.nn.softmax(g_logits, axis=-1)
    g_idx = jnp.argmax(g_logits, axis=-1)
    g_w = jnp.take_along_axis(g_prob, g_idx[:, None], axis=-1)
    e_logits = ((hf @ w_re).astype(jnp.float32) + b_re.astype(jnp.float32)).reshape(
        -1, N_GROUPS, EXPERTS_PER_GROUP)
    e_in_group = jnp.take_along_axis(e_logits, g_idx[:, None, None], axis=1)[:, 0]
    top_v, top_i = lax.top_k(e_in_group, TOP_K)
    top_w = jax.nn.softmax(top_v, axis=-1) * g_w
    expert_id = g_idx[:, None] * EXPERTS_PER_GROUP + top_i
    gates = jnp.einsum("nk,nke->ne", top_w, jax.nn.one_hot(expert_id, N_EXPERTS, dtype=jnp.float32))
    hid = jax.nn.silu(jnp.einsum("nd,edf->nef", hf, w_gate)) * jnp.einsum("nd,edf->nef", hf, w_up)
    y = jnp.einsum("nef,efd->nd", hid * gates.astype(hid.dtype)[..., None], w_down)
    return y.reshape(B, T, D)


def decoder_layer(x, offset, past, g_mix, mix_w, g_ffn, moe_w):
    mix, new_rows = token_mixer(rmsnorm(x, g_mix), offset, past, *mix_w)
    x = x + mix
    x = x + hier_moe(rmsnorm(x, g_ffn), *moe_w)
    return x, new_rows


def gather_pages(pool, page_table):
    g = pool[page_table]
    return g.reshape((page_table.shape[0], page_table.shape[1] * pool.shape[1]) + pool.shape[2:])


def setup_inputs(seed: int = 0) -> dict:
    key = jax.random.key(seed)
    ks = jax.random.split(key, 32)
    f32 = jnp.float32

    def nrm(k, shape, scale=1.0):
        return jax.random.normal(k, shape, f32) * scale

    n_pages = PAST_LEN // PAGE_SIZE
    n_used = DEC_BATCH * n_pages
    n_pool = n_used + (n_used + 3) // 4
    page_table = jax.random.permutation(ks[6], n_pool)[:n_used].reshape(DEC_BATCH, n_pages).astype(jnp.int32)
    D = D_MODEL
    return {
        "x_prompt": nrm(ks[0], (BATCH, SEQ, D)),
        "x_sample": nrm(ks[1], (DEC_BATCH, DEC_SEQ, D)),
        "cache_ckv": nrm(ks[2], (DEPTH, n_pool, PAGE_SIZE, KV_LORA)),
        "cache_kpe": nrm(ks[3], (DEPTH, n_pool, PAGE_SIZE, QK_ROPE)),
        "cache_sb_k": nrm(ks[4], (DEPTH, n_pool, PAGE_SIZE, SB_KV_HEADS, SB_HEAD_DIM)),
        "cache_sb_v": nrm(ks[5], (DEPTH, n_pool, PAGE_SIZE, SB_KV_HEADS, SB_HEAD_DIM)),
        "page_table": page_table,
        "norm_mix": 1.0 + nrm(ks[7], (DEPTH, D), 0.02),
        "w_in": nrm(ks[8], (DEPTH, D, IN_WIDTH), D ** -0.5),
        "q_norm": 1.0 + nrm(ks[9], (DEPTH, Q_LORA), 0.02),
        "w_uq": nrm(ks[10], (DEPTH, Q_LORA, MLA_HEADS, QK_NOPE + QK_ROPE), Q_LORA ** -0.5),
        "kv_norm": 1.0 + nrm(ks[11], (DEPTH, KV_LORA), 0.02),
        "w_uk": nrm(ks[12], (DEPTH, KV_LORA, MLA_HEADS, QK_NOPE), KV_LORA ** -0.5),
        "w_uv": nrm(ks[13], (DEPTH, KV_LORA, MLA_HEADS, V_HEAD), KV_LORA ** -0.5),
        "w_pa": nrm(ks[14], (DEPTH, MLA_HEADS * V_HEAD, D), (MLA_HEADS * V_HEAD) ** -0.5),
        "w_pb": nrm(ks[15], (DEPTH, SB_HEADS * SB_HEAD_DIM, D), (SB_HEADS * SB_HEAD_DIM) ** -0.5),
        "w_o": nrm(ks[16], (DEPTH, D, D), D ** -0.5),
        "norm_ffn": 1.0 + nrm(ks[17], (DEPTH, D), 0.02),
        "w_rg": nrm(ks[18], (DEPTH, D, N_GROUPS), D ** -0.5),
        "b_rg": nrm(ks[19], (DEPTH, N_GROUPS), 0.01),
        "w_re": nrm(ks[20], (DEPTH, D, N_EXPERTS), D ** -0.5),
        "b_re": nrm(ks[21], (DEPTH, N_EXPERTS), 0.01),
        "w_gate": nrm(ks[22], (DEPTH, N_EXPERTS, D, EXPERT_FF), D ** -0.5),
        "w_up": nrm(ks[23], (DEPTH, N_EXPERTS, D, EXPERT_FF), D ** -0.5),
        "w_down": nrm(ks[24], (DEPTH, N_EXPERTS, EXPERT_FF, D), EXPERT_FF ** -0.5),
        "norm_final": 1.0 + nrm(ks[25], (D,), 0.02),
    }


def reference(x_prompt, x_sample, cache_ckv, cache_kpe, cache_sb_k, cache_sb_v, page_table,
              norm_mix, w_in, q_norm, w_uq, kv_norm, w_uk, w_uv, w_pa, w_pb, w_o,
              norm_ffn, w_rg, b_rg, w_re, b_re, w_gate, w_up, w_down, norm_final):
    past_len = page_table.shape[1] * cache_ckv.shape[2]
    hp, hs = x_prompt, x_sample
    rows_p, rows_s = [], []
    for l in range(DEPTH):
        mix_w = (w_in[l], q_norm[l], w_uq[l], kv_norm[l], w_uk[l], w_uv[l], w_pa[l], w_pb[l], w_o[l])
        moe_w = (w_rg[l], b_rg[l], w_re[l], b_re[l], w_gate[l], w_up[l], w_down[l])
        past = (gather_pages(cache_ckv[l], page_table), gather_pages(cache_kpe[l], page_table),
                gather_pages(cache_sb_k[l], page_table), gather_pages(cache_sb_v[l], page_table))
        hp, new_p = decoder_layer(hp, 0, None, norm_mix[l], mix_w, norm_ffn[l], moe_w)
        hs, new_s = decoder_layer(hs, past_len, past, norm_mix[l], mix_w, norm_ffn[l], moe_w)
        rows_p.append(new_p)
        rows_s.append(new_s)
    y_prompt = rmsnorm(hp, norm_final)
    y_sample = rmsnorm(hs, norm_final)
    ckv_p, kpe_p, sbk_p, sbv_p = [jnp.stack([r[i] for r in rows_p]) for i in range(4)]
    ckv_s, kpe_s, sbk_s, sbv_s = [jnp.stack([r[i] for r in rows_s]) for i in range(4)]
    return (y_prompt, y_sample, ckv_p, kpe_p, sbk_p, sbv_p, ckv_s, kpe_s, sbk_s, sbv_s)
```

```python
import functools

import jax
import jax.numpy as jnp
import numpy as np
from jax import lax
from jax.experimental import pallas as pl
from jax.experimental.pallas import tpu as pltpu

F32 = jnp.float32
BF16 = jnp.bfloat16

EPS = 1e-6
ROPE_THETA = 10000.0
MLA_HEADS = 8
Q_LORA = 512
KV_LORA = 512
QK_NOPE = 128
QK_ROPE = 64
V_HEAD = 128
MLA_SCALE = (QK_NOPE + QK_ROPE) ** -0.5
SB_HEADS = 8
SB_KV_HEADS = 2
SB_GROUP = SB_HEADS // SB_KV_HEADS
SB_HEAD_DIM = 128
SB_SCALE = SB_HEAD_DIM ** -0.5
N_GROUPS = 4
EXPERTS_PER_GROUP = 8
N_EXPERTS = N_GROUPS * EXPERTS_PER_GROUP
TOP_K = 2
NEG_BIG = -1e30

COL_GATE_A = 0
COL_GATE_B = 2048
COL_SBQ = 4096
COL_CQ = 5120
COL_CKV = 5632
COL_SBKV = 6144
COL_KPE = 6656
PROJ_WIDTH = 7168

VMEM_LIMIT = 48 * 1024 * 1024


def _cparams(semantics):
    return pltpu.CompilerParams(dimension_semantics=semantics, vmem_limit_bytes=VMEM_LIMIT)


def _dot(a, b):
    return jnp.dot(a, b, preferred_element_type=F32)


def _dot_nt(a, b):
    return lax.dot_general(a, b, (((1,), (1,)), ((), ())), preferred_element_type=F32)


def _rms(x, gain):
    return x * lax.rsqrt(jnp.mean(x * x, axis=-1, keepdims=True) + EPS) * gain


def _sigmoid(x):
    return 1.0 / (1.0 + jnp.exp(-x))


def _inproj_kernel(x_ref, g_ref, w_ref, o_ref, h_scr):
    @pl.when(pl.program_id(1) == 0)
    def _():
        h_scr[...] = _rms(x_ref[...], g_ref[...]).astype(BF16)

    o_ref[...] = _dot(h_scr[...], w_ref[...])


def _inproj(x, gain, w1, tm=512, tn=1024):
    n, d = x.shape
    width = w1.shape[1]
    return pl.pallas_call(
        _inproj_kernel,
        grid=(n // tm, width // tn),
        in_specs=[pl.BlockSpec((tm, d), lambda i, j: (i, 0)),
                  pl.BlockSpec((1, d), lambda i, j: (0, 0)),
                  pl.BlockSpec((d, tn), lambda i, j: (0, j))],
        out_specs=pl.BlockSpec((tm, tn), lambda i, j: (i, j)),
        out_shape=jax.ShapeDtypeStruct((n, width), F32),
        scratch_shapes=[pltpu.VMEM((tm, d), BF16)],
        compiler_params=_cparams(("parallel", "arbitrary")),
        name="inproj",
    )(x, gain, w1)


def _post_kernel(cq_ref, ckv_ref, sbkv_ref, kpe_ref, sbq_ref, qn_ref, kvn_ref, wq_ref, wuk_ref,
                 cosq_ref, sinq_ref, csk_ref,
                 ckv_o, kpe_o, sbk_o, sbv_o, ckvb_o, kpeb_o, sbkb_o, sbvb_o, qlat_o, qpe_o, sbqb_o):
    cqn = _rms(cq_ref[...], qn_ref[...]).astype(BF16)
    q = _dot(cqn, wq_ref[...])
    for h in range(MLA_HEADS):
        q_nope = q[:, QK_NOPE * h:QK_NOPE * (h + 1)].astype(BF16)
        qlat_o[h] = _dot(q_nope, wuk_ref[h]).astype(BF16)
    pe0 = MLA_HEADS * QK_NOPE
    pe1 = pe0 + MLA_HEADS * QK_ROPE
    roped = q[:, pe0:pe1] * cosq_ref[...] + q[:, pe1:pe1 + MLA_HEADS * QK_ROPE] * sinq_ref[...]
    for h in range(MLA_HEADS):
        qpe_o[h] = roped[:, QK_ROPE * h:QK_ROPE * (h + 1)].astype(BF16)

    ckv = _rms(ckv_ref[...], kvn_ref[...])
    ckv_o[...] = ckv
    ckvb_o[...] = ckv.astype(BF16)

    prod = kpe_ref[...] * csk_ref[...]
    kpe = prod[:, :QK_ROPE] + prod[:, QK_ROPE:]
    kpe_o[...] = kpe
    kpeb_o[...] = kpe.astype(BF16)

    kvw = SB_KV_HEADS * SB_HEAD_DIM
    sbk = sbkv_ref[:, :kvw]
    sbv = sbkv_ref[:, kvw:]
    sbk_o[...] = sbk
    sbv_o[...] = sbv
    sbkb_o[...] = sbk.astype(BF16)
    sbvb_o[...] = sbv.astype(BF16)
    for h in range(SB_HEADS):
        sbqb_o[h] = sbq_ref[:, SB_HEAD_DIM * h:SB_HEAD_DIM * (h + 1)].astype(BF16)


def _post(proj, q_norm, kv_norm, wq, wuk_t, cosq, sinq, csk, n_prompt, seq, tm=256):
    n = proj.shape[0]
    prompt_blocks = n_prompt // tm
    table_blocks = seq // tm

    def tab(i):
        return (jnp.where(i < prompt_blocks, i % table_blocks, table_blocks), 0)

    def col(width, offset):
        return pl.BlockSpec((tm, width), lambda i: (i, offset // width))

    def full(shape):
        return pl.BlockSpec(shape, lambda i: (0,) * len(shape))

    def rows(width, dtype):
        return pl.BlockSpec((tm, width), lambda i: (i, 0)), jax.ShapeDtypeStruct((n, width), dtype)

    def heads(width):
        return (pl.BlockSpec((MLA_HEADS, tm, width), lambda i: (0, i, 0)),
                jax.ShapeDtypeStruct((MLA_HEADS, n, width), BF16))

    kvw = SB_KV_HEADS * SB_HEAD_DIM
    outs = [rows(KV_LORA, F32), rows(QK_ROPE, F32), rows(kvw, F32), rows(kvw, F32),
            rows(KV_LORA, BF16), rows(QK_ROPE, BF16), rows(kvw, BF16), rows(kvw, BF16),
            heads(KV_LORA), heads(QK_ROPE), heads(SB_HEAD_DIM)]
    return pl.pallas_call(
        _post_kernel,
        grid=(n // tm,),
        in_specs=[col(Q_LORA, COL_CQ), col(KV_LORA, COL_CKV), col(2 * kvw, COL_SBKV),
                  col(2 * QK_ROPE, COL_KPE), col(SB_HEADS * SB_HEAD_DIM, COL_SBQ),
                  full((1, Q_LORA)), full((1, KV_LORA)), full(wq.shape), full(wuk_t.shape),
                  pl.BlockSpec((tm, MLA_HEADS * QK_ROPE), tab),
                  pl.BlockSpec((tm, MLA_HEADS * QK_ROPE), tab),
                  pl.BlockSpec((tm, 2 * QK_ROPE), tab)],
        out_specs=[o[0] for o in outs],
        out_shape=[o[1] for o in outs],
        compiler_params=_cparams(("parallel",)),
        name="post_proj",
    )(proj, proj, proj, proj, proj, q_norm, kv_norm, wq, wuk_t, cosq, sinq, csk)


def _mla_prompt_kernel(qlat_ref, qpe_ref, ckv_ref, kpe_ref, wuv_ref, o_ref, m_scr, l_scr, acc_scr, *, tq, tk):
    qi = pl.program_id(1)
    ki = pl.program_id(2)
    rows = MLA_HEADS * tq
    last = ((qi + 1) * tq - 1) // tk

    @pl.when(ki == 0)
    def _():
        m_scr[...] = jnp.full(m_scr.shape, NEG_BIG, F32)
        l_scr[...] = jnp.zeros(l_scr.shape, F32)
        acc_scr[...] = jnp.zeros(acc_scr.shape, F32)

    @pl.when(ki <= last)
    def _():
        q = qlat_ref[...].reshape(rows, KV_LORA)
        qp = qpe_ref[...].reshape(rows, QK_ROPE)
        k = ckv_ref[...]
        s = (_dot_nt(q, k) + _dot_nt(qp, kpe_ref[...])) * MLA_SCALE
        q_pos = qi * tq + (lax.broadcasted_iota(jnp.int32, (rows, tk), 0) & (tq - 1))
        k_pos = ki * tk + lax.broadcasted_iota(jnp.int32, (rows, tk), 1)
        s = jnp.where(k_pos <= q_pos, s, -jnp.inf)
        m_prev = m_scr[...]
        m_new = jnp.maximum(m_prev, jnp.max(s, axis=-1, keepdims=True))
        alpha = jnp.exp(m_prev - m_new)
        p = jnp.exp(s - m_new)
        l_scr[...] = alpha * l_scr[...] + jnp.sum(p, axis=-1, keepdims=True)
        acc_scr[...] = alpha * acc_scr[...] + _dot(p.astype(BF16), k)
        m_scr[...] = m_new

    @pl.when(ki == pl.num_programs(2) - 1)
    def _():
        o_lat = (acc_scr[...] / l_scr[...]).astype(BF16)
        for h in range(MLA_HEADS):
            o_ref[:, V_HEAD * h:V_HEAD * (h + 1)] = _dot(o_lat[h * tq:(h + 1) * tq], wuv_ref[h]).astype(BF16)


def _mla_prompt(qlat, qpe, ckvb, kpeb, wuv, batch, seq, tq=128, tk=512):
    nq = seq // tq
    nk = seq // tk
    rows = MLA_HEADS * tq

    def kmap(b, qi, ki):
        return (b * nk + jnp.minimum(ki, ((qi + 1) * tq - 1) // tk), 0)

    return pl.pallas_call(
        functools.partial(_mla_prompt_kernel, tq=tq, tk=tk),
        grid=(batch, nq, nk),
        in_specs=[pl.BlockSpec((MLA_HEADS, tq, KV_LORA), lambda b, qi, ki: (0, b * nq + qi, 0)),
                  pl.BlockSpec((MLA_HEADS, tq, QK_ROPE), lambda b, qi, ki: (0, b * nq + qi, 0)),
                  pl.BlockSpec((tk, KV_LORA), kmap),
                  pl.BlockSpec((tk, QK_ROPE), kmap),
                  pl.BlockSpec(wuv.shape, lambda b, qi, ki: (0, 0, 0))],
        out_specs=pl.BlockSpec((tq, MLA_HEADS * V_HEAD), lambda b, qi, ki: (b * nq + qi, 0)),
        out_shape=jax.ShapeDtypeStruct((batch * seq, MLA_HEADS * V_HEAD), BF16),
        scratch_shapes=[pltpu.VMEM((rows, 1), F32), pltpu.VMEM((rows, 1), F32), pltpu.VMEM((rows, KV_LORA), F32)],
        compiler_params=_cparams(("parallel", "parallel", "arbitrary")),
        name="mla_prompt",
    )(qlat, qpe, ckvb, kpeb, wuv)


def _sb_block(z, mask, upper, carry):
    soft = jnp.log(1.0 + jnp.exp(-jnp.abs(z)))
    log_keep = -(jnp.maximum(z, 0.0) + soft)
    log_beta = log_keep + z
    if mask is not None:
        log_keep = jnp.where(mask, log_keep, 0.0)
    hi = log_keep.astype(BF16)
    lo = (log_keep - hi.astype(F32)).astype(BF16)
    newer = _dot(hi, upper) + _dot(lo, upper)
    w = jnp.exp(log_beta + newer + carry)
    if mask is not None:
        w = jnp.where(mask, w, 0.0)
    return w, jnp.sum(log_keep, axis=-1, keepdims=True)


def _sb_prompt_kernel(q_ref, k_ref, v_ref, up_ref, o_ref, carry_scr, acc_scr, *, t):
    qi = pl.program_id(2)
    step = pl.program_id(3)
    rows = SB_GROUP * t

    @pl.when(step == 0)
    def _():
        carry_scr[...] = jnp.zeros(carry_scr.shape, F32)
        acc_scr[...] = jnp.zeros(acc_scr.shape, F32)

    def block(masked):
        q = q_ref[...].reshape(rows, SB_HEAD_DIM)
        z = _dot_nt(q, k_ref[...]) * SB_SCALE
        mask = None
        if masked:
            q_pos = lax.broadcasted_iota(jnp.int32, (rows, t), 0) & (t - 1)
            k_pos = lax.broadcasted_iota(jnp.int32, (rows, t), 1)
            mask = k_pos < q_pos
        w, total = _sb_block(z, mask, up_ref[...], carry_scr[...])
        acc_scr[...] += _dot(w.astype(BF16), v_ref[...])
        carry_scr[...] += total

    @pl.when(step == 0)
    def _():
        block(True)

    @pl.when(jnp.logical_and(step > 0, step <= qi))
    def _():
        block(False)

    @pl.when(step == pl.num_programs(3) - 1)
    def _():
        for g in range(SB_GROUP):
            o_ref[:, SB_HEAD_DIM * g:SB_HEAD_DIM * (g + 1)] = acc_scr[g * t:(g + 1) * t].astype(BF16)


def _sb_prompt(sbq, sbkb, sbvb, upper, batch, seq, t=256):
    nq = seq // t
    rows = SB_GROUP * t

    def kvmap(b, kv, qi, step):
        return (b * nq + jnp.maximum(qi - step, 0), kv)

    return pl.pallas_call(
        functools.partial(_sb_prompt_kernel, t=t),
        grid=(batch, SB_KV_HEADS, nq, nq),
        in_specs=[pl.BlockSpec((SB_GROUP, t, SB_HEAD_DIM), lambda b, kv, qi, step: (kv, b * nq + qi, 0)),
                  pl.BlockSpec((t, SB_HEAD_DIM), kvmap),
                  pl.BlockSpec((t, SB_HEAD_DIM), kvmap),
                  pl.BlockSpec((t, t), lambda b, kv, qi, step: (0, 0))],
        out_specs=pl.BlockSpec((t, SB_GROUP * SB_HEAD_DIM), lambda b, kv, qi, step: (b * nq + qi, kv)),
        out_shape=jax.ShapeDtypeStruct((batch * seq, SB_HEADS * SB_HEAD_DIM), BF16),
        scratch_shapes=[pltpu.VMEM((rows, 1), F32), pltpu.VMEM((rows, SB_HEAD_DIM), F32)],
        compiler_params=_cparams(("parallel", "parallel", "parallel", "arbitrary")),
        name="sb_prompt",
    )(sbq, sbkb, sbvb, upper)


def _pad_rows(x, rows):
    return jnp.concatenate([x, jnp.zeros((rows - x.shape[0], x.shape[1]), x.dtype)], axis=0)


def _mla_paged_kernel(pt_ref, q_ref, qpe_ref, ckvn_ref, kpen_ref, *rest, pages, page, t_new):
    ckv_refs = rest[:pages]
    kpe_refs = rest[pages:2 * pages]
    o_ref, m_scr, l_scr, acc_scr = rest[2 * pages:]
    step = pl.program_id(1)
    rows = MLA_HEADS * t_new
    q = q_ref[0]
    qp = qpe_ref[0]

    def update(s, ks):
        m_prev = m_scr[...]
        m_new = jnp.maximum(m_prev, jnp.max(s, axis=-1, keepdims=True))
        alpha = jnp.exp(m_prev - m_new)
        p = jnp.exp(s - m_new)
        l_scr[...] = alpha * l_scr[...] + jnp.sum(p, axis=-1, keepdims=True)
        p = p.astype(BF16)
        pv = _dot(p[:, :page], ks[0])
        for i in range(1, len(ks)):
            pv += _dot(p[:, i * page:(i + 1) * page], ks[i])
        acc_scr[...] = alpha * acc_scr[...] + pv
        m_scr[...] = m_new

    @pl.when(step == 0)
    def _():
        m_scr[...] = jnp.full(m_scr.shape, NEG_BIG, F32)
        l_scr[...] = jnp.zeros(l_scr.shape, F32)
        acc_scr[...] = jnp.zeros(acc_scr.shape, F32)
        k = _pad_rows(ckvn_ref[...], page).astype(BF16)
        kp = _pad_rows(kpen_ref[...], page).astype(BF16)
        s = (_dot_nt(q, k) + _dot_nt(qp, kp)) * MLA_SCALE
        t_q = lax.broadcasted_iota(jnp.int32, (rows, page), 0) & (t_new - 1)
        t_k = lax.broadcasted_iota(jnp.int32, (rows, page), 1)
        update(jnp.where(t_k <= t_q, s, -jnp.inf), [k])

    ks = [r[0].astype(BF16) for r in ckv_refs]
    kps = [r[0].astype(BF16) for r in kpe_refs]
    s = jnp.concatenate([_dot_nt(q, k) + _dot(qp, kp) for k, kp in zip(ks, kps)], axis=1) * MLA_SCALE
    update(s, ks)

    @pl.when(step == pl.num_programs(1) - 1)
    def _():
        o_ref[0] = (acc_scr[...] / l_scr[...]).astype(BF16)


def _mla_paged(page_table, q, qpe, ckv_new, kpe_new, cache_ckv, cache_kpe_t, pages=8):
    batch, rows, _ = q.shape
    t_new = rows // MLA_HEADS
    n_pages = page_table.shape[1]
    page = cache_ckv.shape[1]
    steps = n_pages // pages
    pt = page_table.reshape(-1)

    def pmap(i):
        return lambda b, s, pt_ref: (pt_ref[b * n_pages + s * pages + i], 0, 0)

    grid_spec = pltpu.PrefetchScalarGridSpec(
        num_scalar_prefetch=1,
        grid=(batch, steps),
        in_specs=[pl.BlockSpec((1, rows, KV_LORA), lambda b, s, pt_ref: (b, 0, 0)),
                  pl.BlockSpec((1, rows, QK_ROPE), lambda b, s, pt_ref: (b, 0, 0)),
                  pl.BlockSpec((t_new, KV_LORA), lambda b, s, pt_ref: (b, 0)),
                  pl.BlockSpec((t_new, QK_ROPE), lambda b, s, pt_ref: (b, 0))]
                 + [pl.BlockSpec((1, page, KV_LORA), pmap(i)) for i in range(pages)]
                 + [pl.BlockSpec((1, QK_ROPE, page), pmap(i)) for i in range(pages)],
        out_specs=pl.BlockSpec((1, rows, KV_LORA), lambda b, s, pt_ref: (b, 0, 0)),
        scratch_shapes=[pltpu.VMEM((rows, 1), F32), pltpu.VMEM((rows, 1), F32), pltpu.VMEM((rows, KV_LORA), F32)],
    )
    return pl.pallas_call(
        functools.partial(_mla_paged_kernel, pages=pages, page=page, t_new=t_new),
        grid_spec=grid_spec,
        out_shape=jax.ShapeDtypeStruct((batch, rows, KV_LORA), BF16),
        compiler_params=_cparams(("parallel", "arbitrary")),
        name="mla_paged",
    )(pt, q, qpe, ckv_new, kpe_new, *([cache_ckv] * pages), *([cache_kpe_t] * pages))


def _uv_kernel(o_ref, w_ref, out_ref):
    for h in range(MLA_HEADS):
        out_ref[:, V_HEAD * h:V_HEAD * (h + 1)] = _dot(o_ref[h], w_ref[h]).astype(BF16)


def _uv(o_lat, wuv):
    n = o_lat.shape[1]
    return pl.pallas_call(
        _uv_kernel,
        grid=(1,),
        in_specs=[pl.BlockSpec(o_lat.shape, lambda i: (0, 0, 0)), pl.BlockSpec(wuv.shape, lambda i: (0, 0, 0))],
        out_specs=pl.BlockSpec((n, MLA_HEADS * V_HEAD), lambda i: (0, 0)),
        out_shape=jax.ShapeDtypeStruct((n, MLA_HEADS * V_HEAD), BF16),
        compiler_params=_cparams(("arbitrary",)),
        name="mla_value_up",
    )(o_lat, wuv)


def _sb_paged_kernel(pt_ref, q_ref, kn_ref, vn_ref, up_ref, *rest, pages, page, t_new):
    k_refs = rest[:pages]
    v_refs = rest[pages:2 * pages]
    o_ref, carry_scr, acc_scr = rest[2 * pages:]
    step = pl.program_id(1)
    half = SB_GROUP * t_new
    rows = SB_KV_HEADS * half
    q = q_ref[0]

    def block(k, v, mask):
        z = jnp.concatenate([_dot_nt(q[kv * half:(kv + 1) * half], k[:, kv * SB_HEAD_DIM:(kv + 1) * SB_HEAD_DIM])
                             for kv in range(SB_KV_HEADS)], axis=0) * SB_SCALE
        w, total = _sb_block(z, mask, up_ref[...], carry_scr[...])
        w = w.astype(BF16)
        acc_scr[...] += jnp.concatenate(
            [_dot(w[kv * half:(kv + 1) * half], v[:, kv * SB_HEAD_DIM:(kv + 1) * SB_HEAD_DIM])
             for kv in range(SB_KV_HEADS)], axis=0)
        carry_scr[...] += total

    @pl.when(step == 0)
    def _():
        carry_scr[...] = jnp.zeros(carry_scr.shape, F32)
        acc_scr[...] = jnp.zeros(acc_scr.shape, F32)
        t_q = lax.broadcasted_iota(jnp.int32, (rows, page), 0) & (t_new - 1)
        t_k = lax.broadcasted_iota(jnp.int32, (rows, page), 1)
        block(_pad_rows(kn_ref[...], page).astype(BF16), _pad_rows(vn_ref[...], page).astype(BF16), t_k < t_q)

    def heads(ref):
        return jnp.concatenate([ref[0, pl.ds(kv, page, stride=SB_KV_HEADS), :] for kv in range(SB_KV_HEADS)],
                               axis=1).astype(BF16)

    for i in range(pages - 1, -1, -1):
        block(heads(k_refs[i]), heads(v_refs[i]), None)

    @pl.when(step == pl.num_programs(1) - 1)
    def _():
        o_ref[0] = acc_scr[...].astype(BF16)


def _sb_paged(page_table, q, k_new, v_new, cache_k, cache_v, upper, pages=8):
    batch, rows, _ = q.shape
    t_new = rows // SB_HEADS
    n_pages = page_table.shape[1]
    page = cache_k.shape[1] // SB_KV_HEADS
    steps = n_pages // pages
    kvw = SB_KV_HEADS * SB_HEAD_DIM
    pt = page_table.reshape(-1)

    def pmap(i):
        return lambda b, s, pt_ref: (pt_ref[b * n_pages + (steps - 1 - s) * pages + i], 0, 0)

    grid_spec = pltpu.PrefetchScalarGridSpec(
        num_scalar_prefetch=1,
        grid=(batch, steps),
        in_specs=[pl.BlockSpec((1, rows, SB_HEAD_DIM), lambda b, s, pt_ref: (b, 0, 0)),
                  pl.BlockSpec((t_new, kvw), lambda b, s, pt_ref: (b, 0)),
                  pl.BlockSpec((t_new, kvw), lambda b, s, pt_ref: (b, 0)),
                  pl.BlockSpec((page, page), lambda b, s, pt_ref: (0, 0))]
                 + [pl.BlockSpec((1, page * SB_KV_HEADS, SB_HEAD_DIM), pmap(i)) for i in range(pages)]
                 + [pl.BlockSpec((1, page * SB_KV_HEADS, SB_HEAD_DIM), pmap(i)) for i in range(pages)],
        out_specs=pl.BlockSpec((1, rows, SB_HEAD_DIM), lambda b, s, pt_ref: (b, 0, 0)),
        scratch_shapes=[pltpu.VMEM((rows, 1), F32), pltpu.VMEM((rows, SB_HEAD_DIM), F32)],
    )
    return pl.pallas_call(
        functools.partial(_sb_paged_kernel, pages=pages, page=page, t_new=t_new),
        grid_spec=grid_spec,
        out_shape=jax.ShapeDtypeStruct((batch, rows, SB_HEAD_DIM), BF16),
        compiler_params=_cparams(("parallel", "arbitrary")),
        name="sb_paged",
    )(pt, q, k_new, v_new, upper, *([cache_k] * pages), *([cache_v] * pages))


def _merge_kernel(oa_ref, ob_ref, ga_ref, gb_ref, wpa_ref, wpb_ref, o_ref):
    a = _dot(oa_ref[...], wpa_ref[...])
    b = _dot(ob_ref[...], wpb_ref[...])
    o_ref[...] = (_sigmoid(ga_ref[...]) * a + _sigmoid(gb_ref[...]) * b).astype(BF16)


def _merge(o_a, o_b, proj, wpa, wpb, tm=512, tn=1024):
    n, inner = o_a.shape
    d = wpa.shape[1]
    return pl.pallas_call(
        _merge_kernel,
        grid=(n // tm, d // tn),
        in_specs=[pl.BlockSpec((tm, inner), lambda i, j: (i, 0)),
                  pl.BlockSpec((tm, inner), lambda i, j: (i, 0)),
                  pl.BlockSpec((tm, tn), lambda i, j: (i, COL_GATE_A // tn + j)),
                  pl.BlockSpec((tm, tn), lambda i, j: (i, COL_GATE_B // tn + j)),
                  pl.BlockSpec((inner, tn), lambda i, j: (0, j)),
                  pl.BlockSpec((inner, tn), lambda i, j: (0, j))],
        out_specs=pl.BlockSpec((tm, tn), lambda i, j: (i, j)),
        out_shape=jax.ShapeDtypeStruct((n, d), BF16),
        compiler_params=_cparams(("parallel", "arbitrary")),
        name="gated_merge",
    )(o_a, o_b, proj, proj, wpa, wpb)


def _outproj_kernel(m_ref, x_ref, wo_ref, g_ref, wr_ref, br_ref, x1_ref, h_ref, lg_ref):
    x1 = x_ref[...] + _dot(m_ref[...], wo_ref[...])
    x1_ref[...] = x1
    h = _rms(x1, g_ref[...])
    h_hi = h.astype(BF16)
    h_ref[...] = h_hi
    h_lo = (h - h_hi.astype(F32)).astype(BF16)
    wr = wr_ref[...]
    w_hi = wr.astype(BF16)
    w_lo = (wr - w_hi.astype(F32)).astype(BF16)
    lg_ref[...] = _dot(h_hi, w_hi) + _dot(h_lo, w_hi) + _dot(h_hi, w_lo) + br_ref[...]


def _outproj(merged, x, wo, g_ffn, w_router, b_router, tm=256):
    n, d = x.shape
    rw = w_router.shape[1]
    return pl.pallas_call(
        _outproj_kernel,
        grid=(n // tm,),
        in_specs=[pl.BlockSpec((tm, d), lambda i: (i, 0)),
                  pl.BlockSpec((tm, d), lambda i: (i, 0)),
                  pl.BlockSpec((d, d), lambda i: (0, 0)),
                  pl.BlockSpec((1, d), lambda i: (0, 0)),
                  pl.BlockSpec((d, rw), lambda i: (0, 0)),
                  pl.BlockSpec((1, rw), lambda i: (0, 0))],
        out_specs=[pl.BlockSpec((tm, d), lambda i: (i, 0)),
                   pl.BlockSpec((tm, d), lambda i: (i, 0)),
                   pl.BlockSpec((tm, rw), lambda i: (i, 0))],
        out_shape=[jax.ShapeDtypeStruct((n, d), F32), jax.ShapeDtypeStruct((n, d), BF16),
                   jax.ShapeDtypeStruct((n, rw), F32)],
        compiler_params=_cparams(("parallel",)),
        name="out_proj_router",
    )(merged, x, wo, g_ffn, w_router, b_router)


def _route_kernel(lg_ref, id_ref, w_ref):
    lg = lg_ref[...]
    col = lax.broadcasted_iota(jnp.int32, lg.shape, 1)
    big = jnp.int32(1 << 20)

    def first_argmax(v):
        m = jnp.max(v, axis=-1, keepdims=True)
        return m, jnp.min(jnp.where(v == m, col, big), axis=-1, keepdims=True)

    gl = jnp.where(col < N_GROUPS, lg, -jnp.inf)
    g_max, g_idx = first_argmax(gl)
    g_w = 1.0 / jnp.sum(jnp.exp(gl - g_max), axis=-1, keepdims=True)
    lo = N_GROUPS + g_idx * EXPERTS_PER_GROUP
    el = jnp.where(jnp.logical_and(col >= lo, col < lo + EXPERTS_PER_GROUP), lg, -jnp.inf)
    v1, i1 = first_argmax(el)
    v2, i2 = first_argmax(jnp.where(col == i1, -jnp.inf, el))
    e2 = jnp.exp(v2 - v1)
    w1 = 1.0 / (1.0 + e2) * g_w
    w2 = e2 / (1.0 + e2) * g_w
    id_ref[...] = jnp.where(col == 0, i1 - N_GROUPS, jnp.where(col == 1, i2 - N_GROUPS, 0))
    w_ref[...] = jnp.where(col == 0, w1, jnp.where(col == 1, w2, 0.0))


def _route(logits, tm=512):
    n, rw = logits.shape
    spec = pl.BlockSpec((tm, rw), lambda i: (i, 0))
    return pl.pallas_call(
        _route_kernel,
        grid=(n // tm,),
        in_specs=[spec],
        out_specs=[spec, spec],
        out_shape=[jax.ShapeDtypeStruct((n, rw), jnp.int32), jax.ShapeDtypeStruct((n, rw), F32)],
        compiler_params=_cparams(("parallel",)),
        name="route",
    )(logits)


def _moe_kernel(te_ref, nu_ref, x_ref, rw_ref, wg_ref, wu_ref, wd_ref, o_ref):
    tile = pl.program_id(0)

    @pl.when(tile < nu_ref[0])
    def _():
        x = x_ref[...]
        g = _dot(x, wg_ref[0])
        u = _dot(x, wu_ref[0])
        hid = (g * _sigmoid(g)) * u * rw_ref[...]
        o_ref[...] = _dot(hid.astype(BF16), wd_ref[0])

    @pl.when(tile >= nu_ref[0])
    def _():
        o_ref[...] = jnp.zeros(o_ref.shape, F32)


def _moe(tile_expert, n_used, xs, row_w, wg, wu, wd, tm):
    rows, d = xs.shape
    ff = wg.shape[2]
    grid_spec = pltpu.PrefetchScalarGridSpec(
        num_scalar_prefetch=2,
        grid=(rows // tm,),
        in_specs=[pl.BlockSpec((tm, d), lambda t, te, nu: (t, 0)),
                  pl.BlockSpec((tm, 1), lambda t, te, nu: (t, 0)),
                  pl.BlockSpec((1, d, ff), lambda t, te, nu: (te[t], 0, 0)),
                  pl.BlockSpec((1, d, ff), lambda t, te, nu: (te[t], 0, 0)),
                  pl.BlockSpec((1, ff, d), lambda t, te, nu: (te[t], 0, 0))],
        out_specs=pl.BlockSpec((tm, d), lambda t, te, nu: (t, 0)),
    )
    return pl.pallas_call(
        _moe_kernel,
        grid_spec=grid_spec,
        out_shape=jax.ShapeDtypeStruct((rows, d), F32),
        compiler_params=_cparams(("arbitrary",)),
        name="moe_experts",
    )(tile_expert, n_used, xs, row_w, wg, wu, wd)


def _final_kernel(x_ref, y0_ref, y1_ref, g_ref, o_ref):
    o_ref[...] = _rms(x_ref[...] + (y0_ref[...] + y1_ref[...]), g_ref[...])


def _final(x1, y0, y1, gain, tm=512):
    n, d = x1.shape
    spec = pl.BlockSpec((tm, d), lambda i: (i, 0))
    return pl.pallas_call(
        _final_kernel,
        grid=(n // tm,),
        in_specs=[spec, spec, spec, pl.BlockSpec((1, d), lambda i: (0, 0))],
        out_specs=spec,
        out_shape=jax.ShapeDtypeStruct((n, d), F32),
        compiler_params=_cparams(("parallel",)),
        name="final_norm",
    )(x1, y0, y1, gain)


def _rotate_half_cols(w):
    half = QK_ROPE // 2
    return jnp.concatenate([-w[..., half:], w[..., :half]], axis=-1)


def _prep_w_in(w_in):
    d = w_in.shape[0]
    offs = np.cumsum([0, Q_LORA, KV_LORA, QK_ROPE, SB_HEADS * SB_HEAD_DIM, SB_KV_HEADS * SB_HEAD_DIM,
                      SB_KV_HEADS * SB_HEAD_DIM, d, d])
    c_q, c_kv, k_pe, sb_q, sb_k, sb_v, g_a, g_b = [w_in[:, offs[i]:offs[i + 1]] for i in range(8)]
    parts = [g_a, g_b, sb_q, c_q, c_kv, sb_k, sb_v, k_pe, _rotate_half_cols(k_pe)]
    used = sum(p.shape[1] for p in parts)
    parts.append(jnp.zeros((d, PROJ_WIDTH - used), w_in.dtype))
    return jnp.concatenate(parts, axis=1).astype(BF16)


def _prep_w_uq(w_uq):
    nope = w_uq[:, :, :QK_NOPE].reshape(Q_LORA, MLA_HEADS * QK_NOPE)
    pe = w_uq[:, :, QK_NOPE:]
    pe_rot = _rotate_half_cols(pe)
    return jnp.concatenate([nope, pe.reshape(Q_LORA, -1), pe_rot.reshape(Q_LORA, -1)], axis=1).astype(BF16)


def _rope_tables(positions):
    half = QK_ROPE // 2
    inv = 1.0 / (ROPE_THETA ** (jnp.arange(half, dtype=F32) / half))
    ang = positions.astype(F32)[:, None] * inv[None, :]
    cos, sin = jnp.cos(ang), jnp.sin(ang)
    cos2 = jnp.concatenate([cos, cos], axis=-1)
    sin2 = jnp.concatenate([sin, sin], axis=-1)
    return cos2, sin2


def _strict_upper(n):
    r = np.arange(n)
    return jnp.asarray((r[:, None] > r[None, :]).astype(np.float32), dtype=BF16)


def kernel(x_prompt, x_sample, cache_ckv, cache_kpe, cache_sb_k, cache_sb_v, page_table, norm_mix, w_in, q_norm,
           w_uq, kv_norm, w_uk, w_uv, w_pa, w_pb, w_o, norm_ffn, w_rg, b_rg, w_re, b_re, w_gate, w_up, w_down,
           norm_final):
    depth = w_in.shape[0]
    assert depth == 1
    batch, seq, d = x_prompt.shape
    dec_batch, dec_seq, _ = x_sample.shape
    n_prompt = batch * seq
    n_sample = dec_batch * dec_seq
    n = n_prompt + n_sample
    page = cache_ckv.shape[2]
    past_len = page_table.shape[1] * page
    post_tm = 256
    l = 0

    pos = jnp.concatenate([jnp.arange(seq), past_len + (jnp.arange(post_tm) % dec_seq)])
    cos2, sin2 = _rope_tables(pos)
    cosq = jnp.tile(cos2, (1, MLA_HEADS))
    sinq = jnp.tile(sin2, (1, MLA_HEADS))
    csk = jnp.concatenate([cos2, sin2], axis=1)
    w1 = _prep_w_in(w_in[l])
    wq = _prep_w_uq(w_uq[l])
    wuk_t = jnp.transpose(w_uk[l], (1, 2, 0)).astype(BF16)
    wuv = jnp.transpose(w_uv[l], (1, 0, 2)).astype(BF16)
    rw = 128
    w_router = jnp.concatenate([w_rg[l], w_re[l], jnp.zeros((d, rw - N_GROUPS - N_EXPERTS), F32)], axis=1)
    b_router = jnp.concatenate([b_rg[l], b_re[l], jnp.zeros((rw - N_GROUPS - N_EXPERTS,), F32)])[None, :]

    x = jnp.concatenate([x_prompt.reshape(n_prompt, d), x_sample.reshape(n_sample, d)], axis=0)
    proj = _inproj(x, norm_mix[l][None, :], w1)
    (ckv, kpe, sbk, sbv, ckvb, kpeb, sbkb, sbvb, qlat, qpe, sbq) = _post(
        proj, q_norm[l][None, :], kv_norm[l][None, :], wq, wuk_t, cosq, sinq, csk, n_prompt, seq, tm=post_tm)

    sb_t = 256
    oa_p = _mla_prompt(qlat, qpe, ckvb, kpeb, wuv, batch, seq)
    ob_p = _sb_prompt(sbq, sbkb, sbvb, _strict_upper(sb_t), batch, seq, t=sb_t)

    def per_seq(a):
        a = a[:, n_prompt:].reshape(a.shape[0], dec_batch, dec_seq, a.shape[2])
        return jnp.transpose(a, (1, 0, 2, 3)).reshape(dec_batch, a.shape[0] * dec_seq, a.shape[3])

    n_pool = cache_ckv.shape[1]
    kvw = SB_KV_HEADS * SB_HEAD_DIM
    o_lat_s = _mla_paged(page_table, per_seq(qlat), per_seq(qpe), ckv[n_prompt:], kpe[n_prompt:],
                         cache_ckv[l], jnp.swapaxes(cache_kpe[l], 1, 2))
    o_lat_s = jnp.transpose(o_lat_s.reshape(dec_batch, MLA_HEADS, dec_seq, KV_LORA), (1, 0, 2, 3))
    oa_s = _uv(o_lat_s.reshape(MLA_HEADS, n_sample, KV_LORA), wuv)
    ob_s = _sb_paged(page_table, per_seq(sbq), sbk[n_prompt:], sbv[n_prompt:],
                     cache_sb_k[l].reshape(n_pool, page * SB_KV_HEADS, SB_HEAD_DIM),
                     cache_sb_v[l].reshape(n_pool, page * SB_KV_HEADS, SB_HEAD_DIM),
                     _strict_upper(page))
    ob_s = jnp.transpose(ob_s.reshape(dec_batch, SB_HEADS, dec_seq, SB_HEAD_DIM), (0, 2, 1, 3))
    ob_s = ob_s.reshape(n_sample, SB_HEADS * SB_HEAD_DIM)

    o_a = jnp.concatenate([oa_p, oa_s], axis=0)
    o_b = jnp.concatenate([ob_p, ob_s], axis=0)
    merged = _merge(o_a, o_b, proj, w_pa[l].astype(BF16), w_pb[l].astype(BF16))
    x1, h2, logits = _outproj(merged, x, w_o[l].astype(BF16), norm_ffn[l][None, :], w_router, b_router)
    ids, wts = _route(logits)

    tm = 256
    n_assign = n * TOP_K
    n_tiles = n_assign // tm + N_EXPERTS
    eid = ids[:, :TOP_K].reshape(n_assign)
    gate = wts[:, :TOP_K].reshape(n_assign)
    order = jnp.argsort(eid, stable=True)
    counts = jnp.sum(eid[:, None] == jnp.arange(N_EXPERTS)[None, :], axis=0)
    padded = ((counts + tm - 1) // tm) * tm
    pad_end = jnp.cumsum(padded)
    pad_start = pad_end - padded
    start = jnp.cumsum(counts) - counts
    eid_sorted = eid[order]
    dest = pad_start[eid_sorted] + (jnp.arange(n_assign) - start[eid_sorted])
    slot = jnp.zeros((n_assign,), jnp.int32).at[order].set(dest.astype(jnp.int32))
    src_tok = jnp.zeros((n_tiles * tm,), jnp.int32).at[dest].set((order // TOP_K).astype(jnp.int32))
    row_w = jnp.zeros((n_tiles * tm,), F32).at[dest].set(gate[order])
    n_used = (pad_end[-1] // tm).astype(jnp.int32)
    tile_start = jnp.minimum(jnp.arange(n_tiles), n_used - 1) * tm
    tile_expert = jnp.searchsorted(pad_end, tile_start, side="right").astype(jnp.int32)

    ys = _moe(tile_expert, n_used[None], h2[src_tok], row_w[:, None],
              w_gate[l].astype(BF16), w_up[l].astype(BF16), w_down[l].astype(BF16), tm)
    slot = slot.reshape(n, TOP_K)
    y = _final(x1, ys[slot[:, 0]], ys[slot[:, 1]], norm_final[None, :])

    def rows_p(a, *tail):
        return a[:n_prompt].reshape((depth, batch, seq) + tail)

    def rows_s(a, *tail):
        return a[n_prompt:].reshape((depth, dec_batch, dec_seq) + tail)

    return (y[:n_prompt].reshape(batch, seq, d), y[n_prompt:].reshape(dec_batch, dec_seq, d),
            rows_p(ckv, KV_LORA), rows_p(kpe, QK_ROPE),
            rows_p(sbk, SB_KV_HEADS, SB_HEAD_DIM), rows_p(sbv, SB_KV_HEADS, SB_HEAD_DIM),
            rows_s(ckv, KV_LORA), rows_s(kpe, QK_ROPE),
            rows_s(sbk, SB_KV_HEADS, SB_HEAD_DIM), rows_s(sbv, SB_KV_HEADS, SB_HEAD_DIM))
```

```python
import functools

import jax
import jax.numpy as jnp
import numpy as np
from jax import lax
from jax.experimental import pallas as pl
from jax.experimental.pallas import tpu as pltpu

F32 = jnp.float32
BF16 = jnp.bfloat16

EPS = 1e-6
ROPE_THETA = 10000.0
MLA_HEADS = 8
Q_LORA = 512
KV_LORA = 512
QK_NOPE = 128
QK_ROPE = 64
V_HEAD = 128
MLA_SCALE = (QK_NOPE + QK_ROPE) ** -0.5
SB_HEADS = 8
SB_KV_HEADS = 2
SB_GROUP = SB_HEADS // SB_KV_HEADS
SB_HEAD_DIM = 128
SB_SCALE = SB_HEAD_DIM ** -0.5
N_GROUPS = 4
EXPERTS_PER_GROUP = 8
N_EXPERTS = N_GROUPS * EXPERTS_PER_GROUP
TOP_K = 2
NEG_BIG = -1e30

COL_GATE_A = 0
COL_GATE_B = 2048
COL_SBQ = 4096
COL_CQ = 5120
COL_CKV = 5632
COL_SBKV = 6144
COL_KPE = 6656
PROJ_WIDTH = 7168

VMEM_LIMIT = 48 * 1024 * 1024


def _cparams(semantics):
    return pltpu.CompilerParams(dimension_semantics=semantics, vmem_limit_bytes=VMEM_LIMIT)


def _dot(a, b):
    return jnp.dot(a, b, preferred_element_type=F32)


def _dot_nt(a, b):
    return lax.dot_general(a, b, (((1,), (1,)), ((), ())), preferred_element_type=F32)


def _rms(x, gain):
    return x * lax.rsqrt(jnp.mean(x * x, axis=-1, keepdims=True) + EPS) * gain


def _sigmoid(x):
    return 1.0 / (1.0 + jnp.exp(-x))


def _inproj_kernel(x_ref, g_ref, w_ref, o_ref, h_scr):
    @pl.when(pl.program_id(1) == 0)
    def _():
        h_scr[...] = _rms(x_ref[...], g_ref[...]).astype(BF16)

    o_ref[...] = _dot(h_scr[...], w_ref[...])


def _inproj(x, gain, w1, tm=512, tn=1024):
    n, d = x.shape
    width = w1.shape[1]
    return pl.pallas_call(
        _inproj_kernel,
        grid=(n // tm, width // tn),
        in_specs=[pl.BlockSpec((tm, d), lambda i, j: (i, 0)),
                  pl.BlockSpec((1, d), lambda i, j: (0, 0)),
                  pl.BlockSpec((d, tn), lambda i, j: (0, j))],
        out_specs=pl.BlockSpec((tm, tn), lambda i, j: (i, j)),
        out_shape=jax.ShapeDtypeStruct((n, width), F32),
        scratch_shapes=[pltpu.VMEM((tm, d), BF16)],
        compiler_params=_cparams(("parallel", "arbitrary")),
        name="inproj",
    )(x, gain, w1)


def _post_kernel(cq_ref, ckv_ref, sbkv_ref, kpe_ref, sbq_ref, qn_ref, kvn_ref, wq_ref, wuk_ref,
                 cosq_ref, sinq_ref, csk_ref,
                 ckv_o, kpe_o, sbk_o, sbv_o, ckvb_o, kpeb_o, sbkb_o, sbvb_o, qlat_o, qpe_o, sbqb_o):
    cqn = _rms(cq_ref[...], qn_ref[...]).astype(BF16)
    q = _dot(cqn, wq_ref[...])
    for h in range(MLA_HEADS):
        q_nope = q[:, QK_NOPE * h:QK_NOPE * (h + 1)].astype(BF16)
        qlat_o[h] = _dot(q_nope, wuk_ref[h]).astype(BF16)
    pe0 = MLA_HEADS * QK_NOPE
    pe1 = pe0 + MLA_HEADS * QK_ROPE
    roped = q[:, pe0:pe1] * cosq_ref[...] + q[:, pe1:pe1 + MLA_HEADS * QK_ROPE] * sinq_ref[...]
    for h in range(MLA_HEADS):
        qpe_o[h] = roped[:, QK_ROPE * h:QK_ROPE * (h + 1)].astype(BF16)

    ckv = _rms(ckv_ref[...], kvn_ref[...])
    ckv_o[...] = ckv
    ckvb_o[...] = ckv.astype(BF16)

    prod = kpe_ref[...] * csk_ref[...]
    kpe = prod[:, :QK_ROPE] + prod[:, QK_ROPE:]
    kpe_o[...] = kpe
    kpeb_o[...] = kpe.astype(BF16)

    kvw = SB_KV_HEADS * SB_HEAD_DIM
    sbk = sbkv_ref[:, :kvw]
    sbv = sbkv_ref[:, kvw:]
    sbk_o[...] = sbk
    sbv_o[...] = sbv
    sbkb_o[...] = sbk.astype(BF16)
    sbvb_o[...] = sbv.astype(BF16)
    for h in range(SB_HEADS):
        sbqb_o[h] = sbq_ref[:, SB_HEAD_DIM * h:SB_HEAD_DIM * (h + 1)].astype(BF16)


def _post(proj, q_norm, kv_norm, wq, wuk_t, cosq, sinq, csk, n_prompt, seq, tm=256):
    n = proj.shape[0]
    prompt_blocks = n_prompt // tm
    table_blocks = seq // tm

    def tab(i):
        return (jnp.where(i < prompt_blocks, i % table_blocks, table_blocks), 0)

    def col(width, offset):
        return pl.BlockSpec((tm, width), lambda i: (i, offset // width))

    def full(shape):
        return pl.BlockSpec(shape, lambda i: (0,) * len(shape))

    def rows(width, dtype):
        return pl.BlockSpec((tm, width), lambda i: (i, 0)), jax.ShapeDtypeStruct((n, width), dtype)

    def heads(width):
        return (pl.BlockSpec((MLA_HEADS, tm, width), lambda i: (0, i, 0)),
                jax.ShapeDtypeStruct((MLA_HEADS, n, width), BF16))

    kvw = SB_KV_HEADS * SB_HEAD_DIM
    outs = [rows(KV_LORA, F32), rows(QK_ROPE, F32), rows(kvw, F32), rows(kvw, F32),
            rows(KV_LORA, BF16), rows(QK_ROPE, BF16), rows(kvw, BF16), rows(kvw, BF16),
            heads(KV_LORA), heads(QK_ROPE), heads(SB_HEAD_DIM)]
    return pl.pallas_call(
        _post_kernel,
        grid=(n // tm,),
        in_specs=[col(Q_LORA, COL_CQ), col(KV_LORA, COL_CKV), col(2 * kvw, COL_SBKV),
                  col(2 * QK_ROPE, COL_KPE), col(SB_HEADS * SB_HEAD_DIM, COL_SBQ),
                  full((1, Q_LORA)), full((1, KV_LORA)), full(wq.shape), full(wuk_t.shape),
                  pl.BlockSpec((tm, MLA_HEADS * QK_ROPE), tab),
                  pl.BlockSpec((tm, MLA_HEADS * QK_ROPE), tab),
                  pl.BlockSpec((tm, 2 * QK_ROPE), tab)],
        out_specs=[o[0] for o in outs],
        out_shape=[o[1] for o in outs],
        compiler_params=_cparams(("parallel",)),
        name="post_proj",
    )(proj, proj, proj, proj, proj, q_norm, kv_norm, wq, wuk_t, cosq, sinq, csk)


def _mla_prompt_kernel(qlat_ref, qpe_ref, ckv_ref, kpe_ref, wuv_ref, o_ref, m_scr, l_scr, acc_scr, *, tq, tk):
    qi = pl.program_id(1)
    ki = pl.program_id(2)
    rows = MLA_HEADS * tq
    last = ((qi + 1) * tq - 1) // tk

    @pl.when(ki == 0)
    def _():
        m_scr[...] = jnp.full(m_scr.shape, NEG_BIG, F32)
        l_scr[...] = jnp.zeros(l_scr.shape, F32)
        acc_scr[...] = jnp.zeros(acc_scr.shape, F32)

    @pl.when(ki <= last)
    def _():
        q = qlat_ref[...].reshape(rows, KV_LORA)
        qp = qpe_ref[...].reshape(rows, QK_ROPE)
        k = ckv_ref[...]
        s = (_dot_nt(q, k) + _dot_nt(qp, kpe_ref[...])) * MLA_SCALE
        q_pos = qi * tq + (lax.broadcasted_iota(jnp.int32, (rows, tk), 0) & (tq - 1))
        k_pos = ki * tk + lax.broadcasted_iota(jnp.int32, (rows, tk), 1)
        s = jnp.where(k_pos <= q_pos, s, -jnp.inf)
        m_prev = m_scr[...]
        m_new = jnp.maximum(m_prev, jnp.max(s, axis=-1, keepdims=True))
        alpha = jnp.exp(m_prev - m_new)
        p = jnp.exp(s - m_new)
        l_scr[...] = alpha * l_scr[...] + jnp.sum(p, axis=-1, keepdims=True)
        acc_scr[...] = alpha * acc_scr[...] + _dot(p.astype(BF16), k)
        m_scr[...] = m_new

    @pl.when(ki == pl.num_programs(2) - 1)
    def _():
        o_lat = (acc_scr[...] / l_scr[...]).astype(BF16)
        for h in range(MLA_HEADS):
            o_ref[:, V_HEAD * h:V_HEAD * (h + 1)] = _dot(o_lat[h * tq:(h + 1) * tq], wuv_ref[h]).astype(BF16)


def _mla_prompt(qlat, qpe, ckvb, kpeb, wuv, batch, seq, tq=128, tk=512):
    nq = seq // tq
    nk = seq // tk
    rows = MLA_HEADS * tq

    def kmap(b, qi, ki):
        return (b * nk + jnp.minimum(ki, ((qi + 1) * tq - 1) // tk), 0)

    return pl.pallas_call(
        functools.partial(_mla_prompt_kernel, tq=tq, tk=tk),
        grid=(batch, nq, nk),
        in_specs=[pl.BlockSpec((MLA_HEADS, tq, KV_LORA), lambda b, qi, ki: (0, b * nq + qi, 0)),
                  pl.BlockSpec((MLA_HEADS, tq, QK_ROPE), lambda b, qi, ki: (0, b * nq + qi, 0)),
                  pl.BlockSpec((tk, KV_LORA), kmap),
                  pl.BlockSpec((tk, QK_ROPE), kmap),
                  pl.BlockSpec(wuv.shape, lambda b, qi, ki: (0, 0, 0))],
        out_specs=pl.BlockSpec((tq, MLA_HEADS * V_HEAD), lambda b, qi, ki: (b * nq + qi, 0)),
        out_shape=jax.ShapeDtypeStruct((batch * seq, MLA_HEADS * V_HEAD), BF16),
        scratch_shapes=[pltpu.VMEM((rows, 1), F32), pltpu.VMEM((rows, 1), F32), pltpu.VMEM((rows, KV_LORA), F32)],
        compiler_params=_cparams(("parallel", "parallel", "arbitrary")),
        name="mla_prompt",
    )(qlat, qpe, ckvb, kpeb, wuv)


def _sb_block(z, mask, upper, carry):
    soft = jnp.log(1.0 + jnp.exp(-jnp.abs(z)))
    log_keep = -(jnp.maximum(z, 0.0) + soft)
    log_beta = log_keep + z
    if mask is not None:
        log_keep = jnp.where(mask, log_keep, 0.0)
    hi = log_keep.astype(BF16)
    lo = (log_keep - hi.astype(F32)).astype(BF16)
    newer = _dot(hi, upper) + _dot(lo, upper)
    w = jnp.exp(log_beta + newer + carry)
    if mask is not None:
        w = jnp.where(mask, w, 0.0)
    return w, jnp.sum(log_keep, axis=-1, keepdims=True)


def _sb_prompt_kernel(q_ref, k_ref, v_ref, up_ref, o_ref, carry_scr, acc_scr, *, t):
    qi = pl.program_id(2)
    step = pl.program_id(3)
    rows = SB_GROUP * t

    @pl.when(step == 0)
    def _():
        carry_scr[...] = jnp.zeros(carry_scr.shape, F32)
        acc_scr[...] = jnp.zeros(acc_scr.shape, F32)

    def block(masked):
        q = q_ref[...].reshape(rows, SB_HEAD_DIM)
        z = _dot_nt(q, k_ref[...]) * SB_SCALE
        mask = None
        if masked:
            q_pos = lax.broadcasted_iota(jnp.int32, (rows, t), 0) & (t - 1)
            k_pos = lax.broadcasted_iota(jnp.int32, (rows, t), 1)
            mask = k_pos < q_pos
        w, total = _sb_block(z, mask, up_ref[...], carry_scr[...])
        acc_scr[...] += _dot(w.astype(BF16), v_ref[...])
        carry_scr[...] += total

    @pl.when(step == 0)
    def _():
        block(True)

    @pl.when(jnp.logical_and(step > 0, step <= qi))
    def _():
        block(False)

    @pl.when(step == pl.num_programs(3) - 1)
    def _():
        for g in range(SB_GROUP):
            o_ref[:, SB_HEAD_DIM * g:SB_HEAD_DIM * (g + 1)] = acc_scr[g * t:(g + 1) * t].astype(BF16)


def _sb_prompt(sbq, sbkb, sbvb, upper, batch, seq, t=256):
    nq = seq // t
    rows = SB_GROUP * t

    def kvmap(b, kv, qi, step):
        return (b * nq + jnp.maximum(qi - step, 0), kv)

    return pl.pallas_call(
        functools.partial(_sb_prompt_kernel, t=t),
        grid=(batch, SB_KV_HEADS, nq, nq),
        in_specs=[pl.BlockSpec((SB_GROUP, t, SB_HEAD_DIM), lambda b, kv, qi, step: (kv, b * nq + qi, 0)),
                  pl.BlockSpec((t, SB_HEAD_DIM), kvmap),
                  pl.BlockSpec((t, SB_HEAD_DIM), kvmap),
                  pl.BlockSpec((t, t), lambda b, kv, qi, step: (0, 0))],
        out_specs=pl.BlockSpec((t, SB_GROUP * SB_HEAD_DIM), lambda b, kv, qi, step: (b * nq + qi, kv)),
        out_shape=jax.ShapeDtypeStruct((batch * seq, SB_HEADS * SB_HEAD_DIM), BF16),
        scratch_shapes=[pltpu.VMEM((rows, 1), F32), pltpu.VMEM((rows, SB_HEAD_DIM), F32)],
        compiler_params=_cparams(("parallel", "parallel", "parallel", "arbitrary")),
        name="sb_prompt",
    )(sbq, sbkb, sbvb, upper)


def _pad_rows(x, rows):
    return jnp.concatenate([x, jnp.zeros((rows - x.shape[0], x.shape[1]), x.dtype)], axis=0)


def _mla_paged_kernel(pt_ref, q_ref, qpe_ref, w2_ref, ckvn_ref, kpen_ref, *rest, pages, page, t_new):
    ckv_refs = rest[:pages]
    kpe_refs = rest[pages:2 * pages]
    o_ref, m_scr, l_scr, acc_scr = rest[2 * pages:]
    step = pl.program_id(1)
    rows = MLA_HEADS * t_new
    qp = qpe_ref[0]

    def update(s, kcat):
        width = s.shape[1] // len(kcat)
        m_prev = m_scr[...]
        m_new = jnp.maximum(m_prev, jnp.max(s, axis=-1, keepdims=True))
        alpha = jnp.exp(m_prev - m_new)
        p = jnp.exp(s - m_new)
        l_scr[...] = alpha * l_scr[...] + jnp.sum(p, axis=-1, keepdims=True)
        p = p.astype(BF16)
        pv = _dot(p[:, :width], kcat[0])
        for i in range(1, len(kcat)):
            pv += _dot(p[:, i * width:(i + 1) * width], kcat[i])
        acc_scr[...] = alpha * acc_scr[...] + pv
        m_scr[...] = m_new

    @pl.when(step == 0)
    def _():
        m_scr[...] = jnp.full(m_scr.shape, NEG_BIG, F32)
        l_scr[...] = jnp.zeros(l_scr.shape, F32)
        acc_scr[...] = jnp.zeros(acc_scr.shape, F32)
        k = _pad_rows(ckvn_ref[...], page).astype(BF16)
        kp = _pad_rows(kpen_ref[...], page).astype(BF16)
        s = (_dot_nt(q_ref[0], k) + _dot_nt(qp, kp)) * MLA_SCALE
        t_q = lax.broadcasted_iota(jnp.int32, (rows, page), 0) & (t_new - 1)
        t_k = lax.broadcasted_iota(jnp.int32, (rows, page), 1)
        update(jnp.where(t_k <= t_q, s, -jnp.inf), [k])

    ks = [r[0].astype(BF16) for r in ckv_refs]
    pairs = pages // 2
    lhs = jnp.concatenate([jnp.concatenate([ks[2 * j], ks[2 * j + 1]], axis=1) for j in range(pairs)], axis=0)
    st = _dot(lhs, w2_ref[0])
    nope = []
    for j in range(pairs):
        both = st[j * page:(j + 1) * page].T
        nope += [both[:rows], both[rows:]]
    s = jnp.concatenate([nope[i] + _dot(qp, kpe_refs[i][0].astype(BF16)) for i in range(pages)], axis=1) * MLA_SCALE
    update(s, [jnp.concatenate([ks[2 * j], ks[2 * j + 1]], axis=0) for j in range(pairs)])

    @pl.when(step == pl.num_programs(1) - 1)
    def _():
        o_ref[0] = (acc_scr[...] / l_scr[...]).astype(BF16)


def _mla_paged(page_table, q, qpe, ckv_new, kpe_new, cache_ckv, cache_kpe_t, pages=16):
    batch, rows, _ = q.shape
    t_new = rows // MLA_HEADS
    n_pages = page_table.shape[1]
    page = cache_ckv.shape[1]
    assert 2 * rows == page and n_pages % pages == 0 and pages % 2 == 0
    steps = n_pages // pages
    pt = page_table.reshape(-1)
    q_t = jnp.swapaxes(q, 1, 2)
    zero = jnp.zeros_like(q_t)
    w2 = jnp.concatenate([jnp.concatenate([q_t, zero], axis=2), jnp.concatenate([zero, q_t], axis=2)], axis=1)

    def pmap(i):
        return lambda b, s, pt_ref: (pt_ref[b * n_pages + s * pages + i], 0, 0)

    grid_spec = pltpu.PrefetchScalarGridSpec(
        num_scalar_prefetch=1,
        grid=(batch, steps),
        in_specs=[pl.BlockSpec((1, rows, KV_LORA), lambda b, s, pt_ref: (b, 0, 0)),
                  pl.BlockSpec((1, rows, QK_ROPE), lambda b, s, pt_ref: (b, 0, 0)),
                  pl.BlockSpec((1, 2 * KV_LORA, 2 * rows), lambda b, s, pt_ref: (b, 0, 0)),
                  pl.BlockSpec((t_new, KV_LORA), lambda b, s, pt_ref: (b, 0)),
                  pl.BlockSpec((t_new, QK_ROPE), lambda b, s, pt_ref: (b, 0))]
                 + [pl.BlockSpec((1, page, KV_LORA), pmap(i)) for i in range(pages)]
                 + [pl.BlockSpec((1, QK_ROPE, page), pmap(i)) for i in range(pages)],
        out_specs=pl.BlockSpec((1, rows, KV_LORA), lambda b, s, pt_ref: (b, 0, 0)),
        scratch_shapes=[pltpu.VMEM((rows, 1), F32), pltpu.VMEM((rows, 1), F32), pltpu.VMEM((rows, KV_LORA), F32)],
    )
    return pl.pallas_call(
        functools.partial(_mla_paged_kernel, pages=pages, page=page, t_new=t_new),
        grid_spec=grid_spec,
        out_shape=jax.ShapeDtypeStruct((batch, rows, KV_LORA), BF16),
        compiler_params=_cparams(("parallel", "arbitrary")),
        name="mla_paged",
    )(pt, q, qpe, w2, ckv_new, kpe_new, *([cache_ckv] * pages), *([cache_kpe_t] * pages))


def _uv_kernel(o_ref, w_ref, out_ref):
    for h in range(MLA_HEADS):
        out_ref[:, V_HEAD * h:V_HEAD * (h + 1)] = _dot(o_ref[h], w_ref[h]).astype(BF16)


def _uv(o_lat, wuv):
    n = o_lat.shape[1]
    return pl.pallas_call(
        _uv_kernel,
        grid=(1,),
        in_specs=[pl.BlockSpec(o_lat.shape, lambda i: (0, 0, 0)), pl.BlockSpec(wuv.shape, lambda i: (0, 0, 0))],
        out_specs=pl.BlockSpec((n, MLA_HEADS * V_HEAD), lambda i: (0, 0)),
        out_shape=jax.ShapeDtypeStruct((n, MLA_HEADS * V_HEAD), BF16),
        compiler_params=_cparams(("arbitrary",)),
        name="mla_value_up",
    )(o_lat, wuv)


def _sb_paged_kernel(pt_ref, q_ref, kn_ref, vn_ref, up_ref, *rest, pages, page, t_new):
    k_refs = rest[:pages]
    v_refs = rest[pages:2 * pages]
    o_ref, carry_scr, acc_scr = rest[2 * pages:]
    step = pl.program_id(1)
    half = SB_GROUP * t_new
    rows = SB_KV_HEADS * half
    q = q_ref[0]

    def attend(ks, vs, mask):
        n = len(ks)
        z = jnp.concatenate([_dot_nt(q[kv * half:(kv + 1) * half], k[:, kv * SB_HEAD_DIM:(kv + 1) * SB_HEAD_DIM])
                             for k in ks for kv in range(SB_KV_HEADS)], axis=0) * SB_SCALE
        soft = jnp.log(1.0 + jnp.exp(-jnp.abs(z)))
        log_keep = -(jnp.maximum(z, 0.0) + soft)
        log_beta = log_keep + z
        if mask is not None:
            log_keep = jnp.where(mask, log_keep, 0.0)
        hi = log_keep.astype(BF16)
        lo = (log_keep - hi.astype(F32)).astype(BF16)
        newer = _dot(jnp.concatenate([hi, lo], axis=0), up_ref[...])
        newer = newer[:n * rows] + newer[n * rows:]
        total = jnp.sum(log_keep, axis=-1, keepdims=True)
        carry = carry_scr[...]
        carries = []
        for b in range(n):
            carries.append(carry)
            carry = carry + total[b * rows:(b + 1) * rows]
        carry_scr[...] = carry
        w = jnp.exp(log_beta + newer + jnp.concatenate(carries, axis=0))
        if mask is not None:
            w = jnp.where(mask, w, 0.0)
        w = w.astype(BF16)
        pv = _dot(w[:rows], vs[0])
        for b in range(1, n):
            pv += _dot(w[b * rows:(b + 1) * rows], vs[b])
        acc_scr[...] += pv

    @pl.when(step == 0)
    def _():
        carry_scr[...] = jnp.zeros(carry_scr.shape, F32)
        acc_scr[...] = jnp.zeros(acc_scr.shape, F32)
        t_q = lax.broadcasted_iota(jnp.int32, (rows, page), 0) & (t_new - 1)
        t_k = lax.broadcasted_iota(jnp.int32, (rows, page), 1)
        attend([_pad_rows(kn_ref[...], page).astype(BF16)], [_pad_rows(vn_ref[...], page).astype(BF16)], t_k < t_q)

    def heads(ref):
        return jnp.concatenate([ref[0, pl.ds(kv, page, stride=SB_KV_HEADS), :] for kv in range(SB_KV_HEADS)],
                               axis=1).astype(BF16)

    order = range(pages - 1, -1, -1)
    attend([heads(k_refs[i]) for i in order], [heads(v_refs[i]) for i in order], None)

    @pl.when(step == pl.num_programs(1) - 1)
    def _():
        acc = acc_scr[...]
        o_ref[0] = jnp.concatenate([acc[kv * half:(kv + 1) * half, kv * SB_HEAD_DIM:(kv + 1) * SB_HEAD_DIM]
                                    for kv in range(SB_KV_HEADS)], axis=0).astype(BF16)


def _sb_paged(page_table, q, k_new, v_new, cache_k, cache_v, upper, pages=16):
    batch, rows, _ = q.shape
    t_new = rows // SB_HEADS
    n_pages = page_table.shape[1]
    page = cache_k.shape[1] // SB_KV_HEADS
    steps = n_pages // pages
    kvw = SB_KV_HEADS * SB_HEAD_DIM
    pt = page_table.reshape(-1)

    def pmap(i):
        return lambda b, s, pt_ref: (pt_ref[b * n_pages + (steps - 1 - s) * pages + i], 0, 0)

    grid_spec = pltpu.PrefetchScalarGridSpec(
        num_scalar_prefetch=1,
        grid=(batch, steps),
        in_specs=[pl.BlockSpec((1, rows, SB_HEAD_DIM), lambda b, s, pt_ref: (b, 0, 0)),
                  pl.BlockSpec((t_new, kvw), lambda b, s, pt_ref: (b, 0)),
                  pl.BlockSpec((t_new, kvw), lambda b, s, pt_ref: (b, 0)),
                  pl.BlockSpec((page, page), lambda b, s, pt_ref: (0, 0))]
                 + [pl.BlockSpec((1, page * SB_KV_HEADS, SB_HEAD_DIM), pmap(i)) for i in range(pages)]
                 + [pl.BlockSpec((1, page * SB_KV_HEADS, SB_HEAD_DIM), pmap(i)) for i in range(pages)],
        out_specs=pl.BlockSpec((1, rows, SB_HEAD_DIM), lambda b, s, pt_ref: (b, 0, 0)),
        scratch_shapes=[pltpu.VMEM((rows, 1), F32), pltpu.VMEM((rows, kvw), F32)],
    )
    return pl.pallas_call(
        functools.partial(_sb_paged_kernel, pages=pages, page=page, t_new=t_new),
        grid_spec=grid_spec,
        out_shape=jax.ShapeDtypeStruct((batch, rows, SB_HEAD_DIM), BF16),
        compiler_params=_cparams(("parallel", "arbitrary")),
        name="sb_paged",
    )(pt, q, k_new, v_new, upper, *([cache_k] * pages), *([cache_v] * pages))


def _merge_kernel(oa_ref, ob_ref, ga_ref, gb_ref, wpa_ref, wpb_ref, o_ref):
    a = _dot(oa_ref[...], wpa_ref[...])
    b = _dot(ob_ref[...], wpb_ref[...])
    o_ref[...] = (_sigmoid(ga_ref[...]) * a + _sigmoid(gb_ref[...]) * b).astype(BF16)


def _merge(o_a, o_b, proj, wpa, wpb, tm=512, tn=1024):
    n, inner = o_a.shape
    d = wpa.shape[1]
    return pl.pallas_call(
        _merge_kernel,
        grid=(n // tm, d // tn),
        in_specs=[pl.BlockSpec((tm, inner), lambda i, j: (i, 0)),
                  pl.BlockSpec((tm, inner), lambda i, j: (i, 0)),
                  pl.BlockSpec((tm, tn), lambda i, j: (i, COL_GATE_A // tn + j)),
                  pl.BlockSpec((tm, tn), lambda i, j: (i, COL_GATE_B // tn + j)),
                  pl.BlockSpec((inner, tn), lambda i, j: (0, j)),
                  pl.BlockSpec((inner, tn), lambda i, j: (0, j))],
        out_specs=pl.BlockSpec((tm, tn), lambda i, j: (i, j)),
        out_shape=jax.ShapeDtypeStruct((n, d), BF16),
        compiler_params=_cparams(("parallel", "arbitrary")),
        name="gated_merge",
    )(o_a, o_b, proj, proj, wpa, wpb)


def _outproj_kernel(m_ref, x_ref, wo_ref, g_ref, wr_ref, br_ref, x1_ref, h_ref, lg_ref):
    x1 = x_ref[...] + _dot(m_ref[...], wo_ref[...])
    x1_ref[...] = x1
    h = _rms(x1, g_ref[...])
    h_ref[...] = h
    h_hi = h.astype(BF16)
    h_lo = (h - h_hi.astype(F32)).astype(BF16)
    wr = wr_ref[...]
    w_hi = wr.astype(BF16)
    w_lo = (wr - w_hi.astype(F32)).astype(BF16)
    lg_ref[...] = _dot(h_hi, w_hi) + _dot(h_lo, w_hi) + _dot(h_hi, w_lo) + br_ref[...]


def _outproj(merged, x, wo, g_ffn, w_router, b_router, tm=256):
    n, d = x.shape
    rw = w_router.shape[1]
    return pl.pallas_call(
        _outproj_kernel,
        grid=(n // tm,),
        in_specs=[pl.BlockSpec((tm, d), lambda i: (i, 0)),
                  pl.BlockSpec((tm, d), lambda i: (i, 0)),
                  pl.BlockSpec((d, d), lambda i: (0, 0)),
                  pl.BlockSpec((1, d), lambda i: (0, 0)),
                  pl.BlockSpec((d, rw), lambda i: (0, 0)),
                  pl.BlockSpec((1, rw), lambda i: (0, 0))],
        out_specs=[pl.BlockSpec((tm, d), lambda i: (i, 0)),
                   pl.BlockSpec((tm, d), lambda i: (i, 0)),
                   pl.BlockSpec((tm, rw), lambda i: (i, 0))],
        out_shape=[jax.ShapeDtypeStruct((n, d), F32), jax.ShapeDtypeStruct((n, d), F32),
                   jax.ShapeDtypeStruct((n, rw), F32)],
        compiler_params=_cparams(("parallel",)),
        name="out_proj_router",
    )(merged, x, wo, g_ffn, w_router, b_router)


def _route_kernel(lg_ref, id_ref, w_ref):
    lg = lg_ref[...]
    col = lax.broadcasted_iota(jnp.int32, lg.shape, 1)
    big = jnp.int32(1 << 20)

    def first_argmax(v):
        m = jnp.max(v, axis=-1, keepdims=True)
        return m, jnp.min(jnp.where(v == m, col, big), axis=-1, keepdims=True)

    gl = jnp.where(col < N_GROUPS, lg, -jnp.inf)
    g_max, g_idx = first_argmax(gl)
    g_w = 1.0 / jnp.sum(jnp.exp(gl - g_max), axis=-1, keepdims=True)
    lo = N_GROUPS + g_idx * EXPERTS_PER_GROUP
    el = jnp.where(jnp.logical_and(col >= lo, col < lo + EXPERTS_PER_GROUP), lg, -jnp.inf)
    v1, i1 = first_argmax(el)
    v2, i2 = first_argmax(jnp.where(col == i1, -jnp.inf, el))
    e2 = jnp.exp(v2 - v1)
    w1 = 1.0 / (1.0 + e2) * g_w
    w2 = e2 / (1.0 + e2) * g_w
    id_ref[...] = jnp.where(col == 0, i1 - N_GROUPS, jnp.where(col == 1, i2 - N_GROUPS, 0))
    w_ref[...] = jnp.where(col == 0, w1, jnp.where(col == 1, w2, 0.0))


def _route(logits, tm=512):
    n, rw = logits.shape
    spec = pl.BlockSpec((tm, rw), lambda i: (i, 0))
    return pl.pallas_call(
        _route_kernel,
        grid=(n // tm,),
        in_specs=[spec],
        out_specs=[spec, spec],
        out_shape=[jax.ShapeDtypeStruct((n, rw), jnp.int32), jax.ShapeDtypeStruct((n, rw), F32)],
        compiler_params=_cparams(("parallel",)),
        name="route",
    )(logits)


def _moe_kernel(te_ref, nu_ref, x_ref, rw_ref, wg_ref, wu_ref, wd_ref, o_ref):
    tile = pl.program_id(0)

    @pl.when(tile < nu_ref[0])
    def _():
        x = x_ref[...].astype(BF16)
        g = _dot(x, wg_ref[0])
        u = _dot(x, wu_ref[0])
        hid = (g * _sigmoid(g)) * u * rw_ref[...]
        o_ref[...] = _dot(hid.astype(BF16), wd_ref[0])

    @pl.when(tile >= nu_ref[0])
    def _():
        o_ref[...] = jnp.zeros(o_ref.shape, F32)


def _moe(tile_expert, n_used, xs, row_w, wg, wu, wd, tm):
    rows, d = xs.shape
    ff = wg.shape[2]
    grid_spec = pltpu.PrefetchScalarGridSpec(
        num_scalar_prefetch=2,
        grid=(rows // tm,),
        in_specs=[pl.BlockSpec((tm, d), lambda t, te, nu: (t, 0)),
                  pl.BlockSpec((tm, 1), lambda t, te, nu: (t, 0)),
                  pl.BlockSpec((1, d, ff), lambda t, te, nu: (te[t], 0, 0)),
                  pl.BlockSpec((1, d, ff), lambda t, te, nu: (te[t], 0, 0)),
                  pl.BlockSpec((1, ff, d), lambda t, te, nu: (te[t], 0, 0))],
        out_specs=pl.BlockSpec((tm, d), lambda t, te, nu: (t, 0)),
    )
    return pl.pallas_call(
        _moe_kernel,
        grid_spec=grid_spec,
        out_shape=jax.ShapeDtypeStruct((rows, d), F32),
        compiler_params=_cparams(("arbitrary",)),
        name="moe_experts",
    )(tile_expert, n_used, xs, row_w, wg, wu, wd)


def _expert_rows(h, ids, wts, wg, wu, wd, tm=256):
    n = h.shape[0]
    n_assign = n * TOP_K
    n_tiles = n_assign // tm + N_EXPERTS
    eid = ids[:, :TOP_K].reshape(n_assign)
    gate = wts[:, :TOP_K].reshape(n_assign)
    a_idx = jnp.arange(n_assign, dtype=jnp.int32)
    eid_sorted, a_sorted, gate_sorted = lax.sort((eid, a_idx, gate), num_keys=1, is_stable=True)
    counts = jnp.sum(eid[:, None] == jnp.arange(N_EXPERTS)[None, :], axis=0).astype(jnp.int32)
    padded = ((counts + tm - 1) // tm) * tm
    pad_end = jnp.cumsum(padded)
    pad_start = pad_end - padded
    start = jnp.cumsum(counts) - counts
    dest = pad_start[eid_sorted] + (a_idx - start[eid_sorted])
    _, slot = lax.sort((a_sorted, dest), num_keys=1)
    n_used = (pad_end[-1] // tm).astype(jnp.int32)
    tile_start = jnp.minimum(jnp.arange(n_tiles, dtype=jnp.int32), n_used - 1) * tm
    tile_expert = jnp.searchsorted(pad_end, tile_start, side="right").astype(jnp.int32)
    row = jnp.arange(n_tiles * tm, dtype=jnp.int32)
    row_expert = tile_expert[row // tm]
    rank = row - pad_start[row_expert]
    valid = rank < counts[row_expert]
    src = jnp.clip(start[row_expert] + rank, 0, n_assign - 1)
    src_tok = jnp.where(valid, a_sorted[src] // TOP_K, 0)
    row_w = jnp.where(valid, gate_sorted[src], 0.0)
    ys = _moe(tile_expert, n_used[None], h[src_tok], row_w[:, None], wg, wu, wd, tm)
    slot = slot.reshape(n, TOP_K)
    return ys[slot[:, 0]], ys[slot[:, 1]]


def _final_kernel(x_ref, y0_ref, y1_ref, g_ref, o_ref):
    o_ref[...] = _rms(x_ref[...] + (y0_ref[...] + y1_ref[...]), g_ref[...])


def _final(x1, y0, y1, gain, tm=512):
    n, d = x1.shape
    spec = pl.BlockSpec((tm, d), lambda i: (i, 0))
    return pl.pallas_call(
        _final_kernel,
        grid=(n // tm,),
        in_specs=[spec, spec, spec, pl.BlockSpec((1, d), lambda i: (0, 0))],
        out_specs=spec,
        out_shape=jax.ShapeDtypeStruct((n, d), F32),
        compiler_params=_cparams(("parallel",)),
        name="final_norm",
    )(x1, y0, y1, gain)


def _rotate_half_cols(w):
    half = QK_ROPE // 2
    return jnp.concatenate([-w[..., half:], w[..., :half]], axis=-1)


def _prep_w_in(w_in):
    d = w_in.shape[0]
    offs = np.cumsum([0, Q_LORA, KV_LORA, QK_ROPE, SB_HEADS * SB_HEAD_DIM, SB_KV_HEADS * SB_HEAD_DIM,
                      SB_KV_HEADS * SB_HEAD_DIM, d, d])
    c_q, c_kv, k_pe, sb_q, sb_k, sb_v, g_a, g_b = [w_in[:, offs[i]:offs[i + 1]] for i in range(8)]
    parts = [g_a, g_b, sb_q, c_q, c_kv, sb_k, sb_v, k_pe, _rotate_half_cols(k_pe)]
    used = sum(p.shape[1] for p in parts)
    parts.append(jnp.zeros((d, PROJ_WIDTH - used), w_in.dtype))
    return jnp.concatenate(parts, axis=1).astype(BF16)


def _prep_w_uq(w_uq):
    nope = w_uq[:, :, :QK_NOPE].reshape(Q_LORA, MLA_HEADS * QK_NOPE)
    pe = w_uq[:, :, QK_NOPE:]
    pe_rot = _rotate_half_cols(pe)
    return jnp.concatenate([nope, pe.reshape(Q_LORA, -1), pe_rot.reshape(Q_LORA, -1)], axis=1).astype(BF16)


def _rope_tables(positions):
    half = QK_ROPE // 2
    inv = 1.0 / (ROPE_THETA ** (jnp.arange(half, dtype=F32) / half))
    ang = positions.astype(F32)[:, None] * inv[None, :]
    cos, sin = jnp.cos(ang), jnp.sin(ang)
    cos2 = jnp.concatenate([cos, cos], axis=-1)
    sin2 = jnp.concatenate([sin, sin], axis=-1)
    return cos2, sin2


def _strict_upper(n):
    r = np.arange(n)
    return jnp.asarray((r[:, None] > r[None, :]).astype(np.float32), dtype=BF16)


def kernel(x_prompt, x_sample, cache_ckv, cache_kpe, cache_sb_k, cache_sb_v, page_table, norm_mix, w_in, q_norm,
           w_uq, kv_norm, w_uk, w_uv, w_pa, w_pb, w_o, norm_ffn, w_rg, b_rg, w_re, b_re, w_gate, w_up, w_down,
           norm_final):
    depth = w_in.shape[0]
    assert depth == 1
    batch, seq, d = x_prompt.shape
    dec_batch, dec_seq, _ = x_sample.shape
    n_prompt = batch * seq
    n_sample = dec_batch * dec_seq
    n = n_prompt + n_sample
    page = cache_ckv.shape[2]
    past_len = page_table.shape[1] * page
    post_tm = 256
    l = 0

    pos = jnp.concatenate([jnp.arange(seq), past_len + (jnp.arange(post_tm) % dec_seq)])
    cos2, sin2 = _rope_tables(pos)
    cosq = jnp.tile(cos2, (1, MLA_HEADS))
    sinq = jnp.tile(sin2, (1, MLA_HEADS))
    csk = jnp.concatenate([cos2, sin2], axis=1)
    w1 = _prep_w_in(w_in[l])
    wq = _prep_w_uq(w_uq[l])
    wuk_t = jnp.transpose(w_uk[l], (1, 2, 0)).astype(BF16)
    wuv = jnp.transpose(w_uv[l], (1, 0, 2)).astype(BF16)
    rw = 128
    w_router = jnp.concatenate([w_rg[l], w_re[l], jnp.zeros((d, rw - N_GROUPS - N_EXPERTS), F32)], axis=1)
    b_router = jnp.concatenate([b_rg[l], b_re[l], jnp.zeros((rw - N_GROUPS - N_EXPERTS,), F32)])[None, :]

    x = jnp.concatenate([x_prompt.reshape(n_prompt, d), x_sample.reshape(n_sample, d)], axis=0)
    proj = _inproj(x, norm_mix[l][None, :], w1)
    (ckv, kpe, sbk, sbv, ckvb, kpeb, sbkb, sbvb, qlat, qpe, sbq) = _post(
        proj, q_norm[l][None, :], kv_norm[l][None, :], wq, wuk_t, cosq, sinq, csk, n_prompt, seq, tm=post_tm)

    sb_t = 256
    oa_p = _mla_prompt(qlat, qpe, ckvb, kpeb, wuv, batch, seq)
    ob_p = _sb_prompt(sbq, sbkb, sbvb, _strict_upper(sb_t), batch, seq, t=sb_t)

    def per_seq(a):
        a = a[:, n_prompt:].reshape(a.shape[0], dec_batch, dec_seq, a.shape[2])
        return jnp.transpose(a, (1, 0, 2, 3)).reshape(dec_batch, a.shape[0] * dec_seq, a.shape[3])

    n_pool = cache_ckv.shape[1]
    kvw = SB_KV_HEADS * SB_HEAD_DIM
    o_lat_s = _mla_paged(page_table, per_seq(qlat), per_seq(qpe), ckv[n_prompt:], kpe[n_prompt:],
                         cache_ckv[l], jnp.swapaxes(cache_kpe[l], 1, 2))
    o_lat_s = jnp.transpose(o_lat_s.reshape(dec_batch, MLA_HEADS, dec_seq, KV_LORA), (1, 0, 2, 3))
    oa_s = _uv(o_lat_s.reshape(MLA_HEADS, n_sample, KV_LORA), wuv)
    ob_s = _sb_paged(page_table, per_seq(sbq), sbk[n_prompt:], sbv[n_prompt:],
                     cache_sb_k[l].reshape(n_pool, page * SB_KV_HEADS, SB_HEAD_DIM),
                     cache_sb_v[l].reshape(n_pool, page * SB_KV_HEADS, SB_HEAD_DIM),
                     _strict_upper(page))
    ob_s = jnp.transpose(ob_s.reshape(dec_batch, SB_HEADS, dec_seq, SB_HEAD_DIM), (0, 2, 1, 3))
    ob_s = ob_s.reshape(n_sample, SB_HEADS * SB_HEAD_DIM)

    o_a = jnp.concatenate([oa_p, oa_s], axis=0)
    o_b = jnp.concatenate([ob_p, ob_s], axis=0)
    merged = _merge(o_a, o_b, proj, w_pa[l].astype(BF16), w_pb[l].astype(BF16))
    x1, h2, logits = _outproj(merged, x, w_o[l].astype(BF16), norm_ffn[l][None, :], w_router, b_router)
    ids, wts = _route(logits)

    y_first, y_second = _expert_rows(h2, ids, wts, w_gate[l].astype(BF16), w_up[l].astype(BF16),
                                     w_down[l].astype(BF16))
    y = _final(x1, y_first, y_second, norm_final[None, :])

    def rows_p(a, *tail):
        return a[:n_prompt].reshape((depth, batch, seq) + tail)

    def rows_s(a, *tail):
        return a[n_prompt:].reshape((depth, dec_batch, dec_seq) + tail)

    return (y[:n_prompt].reshape(batch, seq, d), y[n_prompt:].reshape(dec_batch, dec_seq, d),
            rows_p(ckv, KV_LORA), rows_p(kpe, QK_ROPE),
            rows_p(sbk, SB_KV_HEADS, SB_HEAD_DIM), rows_p(sbv, SB_KV_HEADS, SB_HEAD_DIM),
            rows_s(ckv, KV_LORA), rows_s(kpe, QK_ROPE),
            rows_s(sbk, SB_KV_HEADS, SB_HEAD_DIM), rows_s(sbv, SB_KV_HEADS, SB_HEAD_DIM))
```

```python
import functools

import jax
import jax.numpy as jnp
import numpy as np
from jax import lax
from jax.experimental import pallas as pl
from jax.experimental.pallas import tpu as pltpu

F32 = jnp.float32
BF16 = jnp.bfloat16

EPS = 1e-6
ROPE_THETA = 10000.0
MLA_HEADS = 8
Q_LORA = 512
KV_LORA = 512
QK_NOPE = 128
QK_ROPE = 64
V_HEAD = 128
MLA_SCALE = (QK_NOPE + QK_ROPE) ** -0.5
SB_HEADS = 8
SB_KV_HEADS = 2
SB_GROUP = SB_HEADS // SB_KV_HEADS
SB_HEAD_DIM = 128
SB_SCALE = SB_HEAD_DIM ** -0.5
N_GROUPS = 4
EXPERTS_PER_GROUP = 8
N_EXPERTS = N_GROUPS * EXPERTS_PER_GROUP
TOP_K = 2
NEG_BIG = -1e30

COL_GATE_A = 0
COL_GATE_B = 2048
COL_SBQ = 4096
COL_CQ = 5120
COL_CKV = 5632
COL_SBKV = 6144
COL_KPE = 6656
PROJ_WIDTH = 7168

VMEM_LIMIT = 48 * 1024 * 1024


def _cparams(semantics):
    return pltpu.CompilerParams(dimension_semantics=semantics, vmem_limit_bytes=VMEM_LIMIT)


def _dot(a, b):
    return jnp.dot(a, b, preferred_element_type=F32)


def _dot_nt(a, b):
    return lax.dot_general(a, b, (((1,), (1,)), ((), ())), preferred_element_type=F32)


def _rms(x, gain):
    return x * lax.rsqrt(jnp.mean(x * x, axis=-1, keepdims=True) + EPS) * gain


def _sigmoid(x):
    return 1.0 / (1.0 + jnp.exp(-x))


def _inproj_kernel(x_ref, g_ref, w_ref, o_ref, h_scr):
    @pl.when(pl.program_id(1) == 0)
    def _():
        h_scr[...] = _rms(x_ref[...], g_ref[...]).astype(BF16)

    o_ref[...] = _dot(h_scr[...], w_ref[...])


def _inproj(x, gain, w1, tm=512, tn=1024):
    n, d = x.shape
    width = w1.shape[1]
    return pl.pallas_call(
        _inproj_kernel,
        grid=(n // tm, width // tn),
        in_specs=[pl.BlockSpec((tm, d), lambda i, j: (i, 0)),
                  pl.BlockSpec((1, d), lambda i, j: (0, 0)),
                  pl.BlockSpec((d, tn), lambda i, j: (0, j))],
        out_specs=pl.BlockSpec((tm, tn), lambda i, j: (i, j)),
        out_shape=jax.ShapeDtypeStruct((n, width), F32),
        scratch_shapes=[pltpu.VMEM((tm, d), BF16)],
        compiler_params=_cparams(("parallel", "arbitrary")),
        name="inproj",
    )(x, gain, w1)


def _post_kernel(cq_ref, ckv_ref, sbkv_ref, kpe_ref, sbq_ref, qn_ref, kvn_ref, wq_ref, wuk_ref,
                 cosq_ref, sinq_ref, csk_ref,
                 ckv_o, kpe_o, sbk_o, sbv_o, ckvb_o, kpeb_o, sbkb_o, sbvb_o, qlat_o, qpe_o, sbqb_o):
    cqn = _rms(cq_ref[...], qn_ref[...]).astype(BF16)
    q = _dot(cqn, wq_ref[...])
    for h in range(MLA_HEADS):
        q_nope = q[:, QK_NOPE * h:QK_NOPE * (h + 1)].astype(BF16)
        qlat_o[h] = _dot(q_nope, wuk_ref[h]).astype(BF16)
    pe0 = MLA_HEADS * QK_NOPE
    pe1 = pe0 + MLA_HEADS * QK_ROPE
    roped = q[:, pe0:pe1] * cosq_ref[...] + q[:, pe1:pe1 + MLA_HEADS * QK_ROPE] * sinq_ref[...]
    for h in range(MLA_HEADS):
        qpe_o[h] = roped[:, QK_ROPE * h:QK_ROPE * (h + 1)].astype(BF16)

    ckv = _rms(ckv_ref[...], kvn_ref[...])
    ckv_o[...] = ckv
    ckvb_o[...] = ckv.astype(BF16)

    prod = kpe_ref[...] * csk_ref[...]
    kpe = prod[:, :QK_ROPE] + prod[:, QK_ROPE:]
    kpe_o[...] = kpe
    kpeb_o[...] = kpe.astype(BF16)

    kvw = SB_KV_HEADS * SB_HEAD_DIM
    sbk = sbkv_ref[:, :kvw]
    sbv = sbkv_ref[:, kvw:]
    sbk_o[...] = sbk
    sbv_o[...] = sbv
    sbkb_o[...] = sbk.astype(BF16)
    sbvb_o[...] = sbv.astype(BF16)
    for h in range(SB_HEADS):
        sbqb_o[h] = sbq_ref[:, SB_HEAD_DIM * h:SB_HEAD_DIM * (h + 1)].astype(BF16)


def _post(proj, q_norm, kv_norm, wq, wuk_t, cosq, sinq, csk, n_prompt, seq, tm=256):
    n = proj.shape[0]
    prompt_blocks = n_prompt // tm
    table_blocks = seq // tm

    def tab(i):
        return (jnp.where(i < prompt_blocks, i % table_blocks, table_blocks), 0)

    def col(width, offset):
        return pl.BlockSpec((tm, width), lambda i: (i, offset // width))

    def full(shape):
        return pl.BlockSpec(shape, lambda i: (0,) * len(shape))

    def rows(width, dtype):
        return pl.BlockSpec((tm, width), lambda i: (i, 0)), jax.ShapeDtypeStruct((n, width), dtype)

    def heads(width):
        return (pl.BlockSpec((MLA_HEADS, tm, width), lambda i: (0, i, 0)),
                jax.ShapeDtypeStruct((MLA_HEADS, n, width), BF16))

    kvw = SB_KV_HEADS * SB_HEAD_DIM
    outs = [rows(KV_LORA, F32), rows(QK_ROPE, F32), rows(kvw, F32), rows(kvw, F32),
            rows(KV_LORA, BF16), rows(QK_ROPE, BF16), rows(kvw, BF16), rows(kvw, BF16),
            heads(KV_LORA), heads(QK_ROPE), heads(SB_HEAD_DIM)]
    return pl.pallas_call(
        _post_kernel,
        grid=(n // tm,),
        in_specs=[col(Q_LORA, COL_CQ), col(KV_LORA, COL_CKV), col(2 * kvw, COL_SBKV),
                  col(2 * QK_ROPE, COL_KPE), col(SB_HEADS * SB_HEAD_DIM, COL_SBQ),
                  full((1, Q_LORA)), full((1, KV_LORA)), full(wq.shape), full(wuk_t.shape),
                  pl.BlockSpec((tm, MLA_HEADS * QK_ROPE), tab),
                  pl.BlockSpec((tm, MLA_HEADS * QK_ROPE), tab),
                  pl.BlockSpec((tm, 2 * QK_ROPE), tab)],
        out_specs=[o[0] for o in outs],
        out_shape=[o[1] for o in outs],
        compiler_params=_cparams(("parallel",)),
        name="post_proj",
    )(proj, proj, proj, proj, proj, q_norm, kv_norm, wq, wuk_t, cosq, sinq, csk)


def _mla_prompt_kernel(qlat_ref, qpe_ref, ckv_ref, kpe_ref, wuv_ref, o_ref, m_scr, l_scr, acc_scr, *, tq, tk):
    qi = pl.program_id(1)
    ki = pl.program_id(2)
    rows = MLA_HEADS * tq
    last = ((qi + 1) * tq - 1) // tk

    @pl.when(ki == 0)
    def _():
        m_scr[...] = jnp.full(m_scr.shape, NEG_BIG, F32)
        l_scr[...] = jnp.zeros(l_scr.shape, F32)
        acc_scr[...] = jnp.zeros(acc_scr.shape, F32)

    @pl.when(ki <= last)
    def _():
        q = qlat_ref[...].reshape(rows, KV_LORA)
        qp = qpe_ref[...].reshape(rows, QK_ROPE)
        k = ckv_ref[...]
        s = (_dot_nt(q, k) + _dot_nt(qp, kpe_ref[...])) * MLA_SCALE
        q_pos = qi * tq + (lax.broadcasted_iota(jnp.int32, (rows, tk), 0) & (tq - 1))
        k_pos = ki * tk + lax.broadcasted_iota(jnp.int32, (rows, tk), 1)
        s = jnp.where(k_pos <= q_pos, s, -jnp.inf)
        m_prev = m_scr[...]
        m_new = jnp.maximum(m_prev, jnp.max(s, axis=-1, keepdims=True))
        alpha = jnp.exp(m_prev - m_new)
        p = jnp.exp(s - m_new)
        l_scr[...] = alpha * l_scr[...] + jnp.sum(p, axis=-1, keepdims=True)
        acc_scr[...] = alpha * acc_scr[...] + _dot(p.astype(BF16), k)
        m_scr[...] = m_new

    @pl.when(ki == pl.num_programs(2) - 1)
    def _():
        o_lat = (acc_scr[...] / l_scr[...]).astype(BF16)
        for h in range(MLA_HEADS):
            o_ref[:, V_HEAD * h:V_HEAD * (h + 1)] = _dot(o_lat[h * tq:(h + 1) * tq], wuv_ref[h]).astype(BF16)


def _mla_prompt(qlat, qpe, ckvb, kpeb, wuv, batch, seq, tq=128, tk=512):
    nq = seq // tq
    nk = seq // tk
    rows = MLA_HEADS * tq

    def kmap(b, qi, ki):
        return (b * nk + jnp.minimum(ki, ((qi + 1) * tq - 1) // tk), 0)

    return pl.pallas_call(
        functools.partial(_mla_prompt_kernel, tq=tq, tk=tk),
        grid=(batch, nq, nk),
        in_specs=[pl.BlockSpec((MLA_HEADS, tq, KV_LORA), lambda b, qi, ki: (0, b * nq + qi, 0)),
                  pl.BlockSpec((MLA_HEADS, tq, QK_ROPE), lambda b, qi, ki: (0, b * nq + qi, 0)),
                  pl.BlockSpec((tk, KV_LORA), kmap),
                  pl.BlockSpec((tk, QK_ROPE), kmap),
                  pl.BlockSpec(wuv.shape, lambda b, qi, ki: (0, 0, 0))],
        out_specs=pl.BlockSpec((tq, MLA_HEADS * V_HEAD), lambda b, qi, ki: (b * nq + qi, 0)),
        out_shape=jax.ShapeDtypeStruct((batch * seq, MLA_HEADS * V_HEAD), BF16),
        scratch_shapes=[pltpu.VMEM((rows, 1), F32), pltpu.VMEM((rows, 1), F32), pltpu.VMEM((rows, KV_LORA), F32)],
        compiler_params=_cparams(("parallel", "parallel", "arbitrary")),
        name="mla_prompt",
    )(qlat, qpe, ckvb, kpeb, wuv)


def _sb_block(z, mask, upper, carry):
    soft = jnp.log(1.0 + jnp.exp(-jnp.abs(z)))
    log_keep = -(jnp.maximum(z, 0.0) + soft)
    log_beta = log_keep + z
    if mask is not None:
        log_keep = jnp.where(mask, log_keep, 0.0)
    hi = log_keep.astype(BF16)
    lo = (log_keep - hi.astype(F32)).astype(BF16)
    newer = _dot(hi, upper) + _dot(lo, upper)
    w = jnp.exp(log_beta + newer + carry)
    if mask is not None:
        w = jnp.where(mask, w, 0.0)
    return w, jnp.sum(log_keep, axis=-1, keepdims=True)


def _sb_prompt_kernel(q_ref, k_ref, v_ref, up_ref, o_ref, carry_scr, acc_scr, *, t):
    qi = pl.program_id(2)
    step = pl.program_id(3)
    rows = SB_GROUP * t

    @pl.when(step == 0)
    def _():
        carry_scr[...] = jnp.zeros(carry_scr.shape, F32)
        acc_scr[...] = jnp.zeros(acc_scr.shape, F32)

    def block(masked):
        q = q_ref[...].reshape(rows, SB_HEAD_DIM)
        z = _dot_nt(q, k_ref[...]) * SB_SCALE
        mask = None
        if masked:
            q_pos = lax.broadcasted_iota(jnp.int32, (rows, t), 0) & (t - 1)
            k_pos = lax.broadcasted_iota(jnp.int32, (rows, t), 1)
            mask = k_pos < q_pos
        w, total = _sb_block(z, mask, up_ref[...], carry_scr[...])
        acc_scr[...] += _dot(w.astype(BF16), v_ref[...])
        carry_scr[...] += total

    @pl.when(step == 0)
    def _():
        block(True)

    @pl.when(jnp.logical_and(step > 0, step <= qi))
    def _():
        block(False)

    @pl.when(step == pl.num_programs(3) - 1)
    def _():
        for g in range(SB_GROUP):
            o_ref[:, SB_HEAD_DIM * g:SB_HEAD_DIM * (g + 1)] = acc_scr[g * t:(g + 1) * t].astype(BF16)


def _sb_prompt(sbq, sbkb, sbvb, upper, batch, seq, t=256):
    nq = seq // t
    rows = SB_GROUP * t

    def kvmap(b, kv, qi, step):
        return (b * nq + jnp.maximum(qi - step, 0), kv)

    return pl.pallas_call(
        functools.partial(_sb_prompt_kernel, t=t),
        grid=(batch, SB_KV_HEADS, nq, nq),
        in_specs=[pl.BlockSpec((SB_GROUP, t, SB_HEAD_DIM), lambda b, kv, qi, step: (kv, b * nq + qi, 0)),
                  pl.BlockSpec((t, SB_HEAD_DIM), kvmap),
                  pl.BlockSpec((t, SB_HEAD_DIM), kvmap),
                  pl.BlockSpec((t, t), lambda b, kv, qi, step: (0, 0))],
        out_specs=pl.BlockSpec((t, SB_GROUP * SB_HEAD_DIM), lambda b, kv, qi, step: (b * nq + qi, kv)),
        out_shape=jax.ShapeDtypeStruct((batch * seq, SB_HEADS * SB_HEAD_DIM), BF16),
        scratch_shapes=[pltpu.VMEM((rows, 1), F32), pltpu.VMEM((rows, SB_HEAD_DIM), F32)],
        compiler_params=_cparams(("parallel", "parallel", "parallel", "arbitrary")),
        name="sb_prompt",
    )(sbq, sbkb, sbvb, upper)


def _pad_rows(x, rows):
    return jnp.concatenate([x, jnp.zeros((rows - x.shape[0], x.shape[1]), x.dtype)], axis=0)


def _mla_paged_kernel(pt_ref, q_ref, qpe_ref, w2_ref, ckvn_ref, kpen_ref, ckv_hbm, kpe_hbm, o_ref,
                      ckv_buf, kpe_buf, sem, m_scr, l_scr, acc_scr, *, pages, page, t_new):
    step = pl.program_id(1)
    steps = pl.num_programs(1)
    g = pl.program_id(0) * steps + step
    last = pl.num_programs(0) * steps - 1
    slot = lax.rem(g, 2)
    rows = MLA_HEADS * t_new
    qp = qpe_ref[0]

    def page_copies(g_idx, slot_idx):
        out = []
        for i in range(pages):
            pid = 0 if g_idx is None else pt_ref[g_idx * pages + i]
            out.append(pltpu.make_async_copy(ckv_hbm.at[pid], ckv_buf.at[slot_idx, i], sem.at[slot_idx, 0]))
            out.append(pltpu.make_async_copy(kpe_hbm.at[pid], kpe_buf.at[slot_idx, i], sem.at[slot_idx, 1]))
        return out

    @pl.when(g == 0)
    def _():
        for c in page_copies(0, 0):
            c.start()

    for c in page_copies(None, slot):
        c.wait()
    for c in page_copies(jnp.minimum(g + 1, last), 1 - slot):
        c.start()

    def update(s, kcat):
        width = s.shape[1] // len(kcat)
        m_prev = m_scr[...]
        m_new = jnp.maximum(m_prev, jnp.max(s, axis=-1, keepdims=True))
        alpha = jnp.exp(m_prev - m_new)
        p = jnp.exp(s - m_new)
        l_scr[...] = alpha * l_scr[...] + jnp.sum(p, axis=-1, keepdims=True)
        p = p.astype(BF16)
        pv = _dot(p[:, :width], kcat[0])
        for i in range(1, len(kcat)):
            pv += _dot(p[:, i * width:(i + 1) * width], kcat[i])
        acc_scr[...] = alpha * acc_scr[...] + pv
        m_scr[...] = m_new

    @pl.when(step == 0)
    def _():
        m_scr[...] = jnp.full(m_scr.shape, NEG_BIG, F32)
        l_scr[...] = jnp.zeros(l_scr.shape, F32)
        acc_scr[...] = jnp.zeros(acc_scr.shape, F32)
        k = _pad_rows(ckvn_ref[...], page).astype(BF16)
        kp = _pad_rows(kpen_ref[...], page).astype(BF16)
        s = (_dot_nt(q_ref[0], k) + _dot_nt(qp, kp)) * MLA_SCALE
        t_q = lax.broadcasted_iota(jnp.int32, (rows, page), 0) & (t_new - 1)
        t_k = lax.broadcasted_iota(jnp.int32, (rows, page), 1)
        update(jnp.where(t_k <= t_q, s, -jnp.inf), [k])

    ks = [ckv_buf[slot, i].astype(BF16) for i in range(pages)]
    kps = [kpe_buf[slot, i].astype(BF16) for i in range(pages)]
    pairs = pages // 2
    lhs = jnp.concatenate([jnp.concatenate([ks[2 * j], ks[2 * j + 1]], axis=1) for j in range(pairs)], axis=0)
    st = _dot(lhs, w2_ref[0])
    nope = []
    for j in range(pairs):
        both = st[j * page:(j + 1) * page].T
        nope += [both[:rows], both[rows:]]
    s = jnp.concatenate([nope[i] + _dot(qp, kps[i]) for i in range(pages)], axis=1) * MLA_SCALE
    update(s, [jnp.concatenate([ks[2 * j], ks[2 * j + 1]], axis=0) for j in range(pairs)])

    @pl.when(step == steps - 1)
    def _():
        o_ref[0] = (acc_scr[...] / l_scr[...]).astype(BF16)

    @pl.when(g == last)
    def _():
        for c in page_copies(None, 1 - slot):
            c.wait()


def _mla_paged(page_table, q, qpe, ckv_new, kpe_new, cache_ckv, cache_kpe_t, pages=16):
    batch, rows, _ = q.shape
    t_new = rows // MLA_HEADS
    n_pages = page_table.shape[1]
    page = cache_ckv.shape[1]
    assert 2 * rows == page and n_pages % pages == 0 and pages % 2 == 0
    steps = n_pages // pages
    pt = page_table.reshape(-1)
    q_t = jnp.swapaxes(q, 1, 2)
    zero = jnp.zeros_like(q_t)
    w2 = jnp.concatenate([jnp.concatenate([q_t, zero], axis=2), jnp.concatenate([zero, q_t], axis=2)], axis=1)

    grid_spec = pltpu.PrefetchScalarGridSpec(
        num_scalar_prefetch=1,
        grid=(batch, steps),
        in_specs=[pl.BlockSpec((1, rows, KV_LORA), lambda b, s, pt_ref: (b, 0, 0)),
                  pl.BlockSpec((1, rows, QK_ROPE), lambda b, s, pt_ref: (b, 0, 0)),
                  pl.BlockSpec((1, 2 * KV_LORA, 2 * rows), lambda b, s, pt_ref: (b, 0, 0)),
                  pl.BlockSpec((t_new, KV_LORA), lambda b, s, pt_ref: (b, 0)),
                  pl.BlockSpec((t_new, QK_ROPE), lambda b, s, pt_ref: (b, 0)),
                  pl.BlockSpec(memory_space=pl.ANY),
                  pl.BlockSpec(memory_space=pl.ANY)],
        out_specs=pl.BlockSpec((1, rows, KV_LORA), lambda b, s, pt_ref: (b, 0, 0)),
        scratch_shapes=[pltpu.VMEM((2, pages, page, KV_LORA), F32), pltpu.VMEM((2, pages, QK_ROPE, page), F32),
                        pltpu.SemaphoreType.DMA((2, 2)),
                        pltpu.VMEM((rows, 1), F32), pltpu.VMEM((rows, 1), F32), pltpu.VMEM((rows, KV_LORA), F32)],
    )
    return pl.pallas_call(
        functools.partial(_mla_paged_kernel, pages=pages, page=page, t_new=t_new),
        grid_spec=grid_spec,
        out_shape=jax.ShapeDtypeStruct((batch, rows, KV_LORA), BF16),
        compiler_params=_cparams(("arbitrary", "arbitrary")),
        name="mla_paged",
    )(pt, q, qpe, w2, ckv_new, kpe_new, cache_ckv, cache_kpe_t)


def _uv_kernel(o_ref, w_ref, out_ref):
    for h in range(MLA_HEADS):
        out_ref[:, V_HEAD * h:V_HEAD * (h + 1)] = _dot(o_ref[h], w_ref[h]).astype(BF16)


def _uv(o_lat, wuv):
    n = o_lat.shape[1]
    return pl.pallas_call(
        _uv_kernel,
        grid=(1,),
        in_specs=[pl.BlockSpec(o_lat.shape, lambda i: (0, 0, 0)), pl.BlockSpec(wuv.shape, lambda i: (0, 0, 0))],
        out_specs=pl.BlockSpec((n, MLA_HEADS * V_HEAD), lambda i: (0, 0)),
        out_shape=jax.ShapeDtypeStruct((n, MLA_HEADS * V_HEAD), BF16),
        compiler_params=_cparams(("arbitrary",)),
        name="mla_value_up",
    )(o_lat, wuv)


def _sb_paged_kernel(pt_ref, q_ref, kn_ref, vn_ref, up_ref, k_hbm, v_hbm, o_ref,
                     k_buf, v_buf, sem, carry_scr, acc_scr, *, pages, page, t_new):
    step = pl.program_id(1)
    steps = pl.num_programs(1)
    g = pl.program_id(0) * steps + step
    last = pl.num_programs(0) * steps - 1
    slot = lax.rem(g, 2)
    half = SB_GROUP * t_new
    rows = SB_KV_HEADS * half
    q = q_ref[0]

    def page_copies(g_idx, slot_idx):
        base = 0
        if g_idx is not None:
            base = (lax.div(g_idx, steps) * steps + (steps - 1 - lax.rem(g_idx, steps))) * pages
        out = []
        for i in range(pages):
            pid = 0 if g_idx is None else pt_ref[base + i]
            out.append(pltpu.make_async_copy(k_hbm.at[pid], k_buf.at[slot_idx, i], sem.at[slot_idx, 0]))
            out.append(pltpu.make_async_copy(v_hbm.at[pid], v_buf.at[slot_idx, i], sem.at[slot_idx, 1]))
        return out

    @pl.when(g == 0)
    def _():
        for c in page_copies(0, 0):
            c.start()

    for c in page_copies(None, slot):
        c.wait()
    for c in page_copies(jnp.minimum(g + 1, last), 1 - slot):
        c.start()

    def attend(ks, vs, mask):
        n = len(ks)
        z = jnp.concatenate([_dot_nt(q[kv * half:(kv + 1) * half], k[:, kv * SB_HEAD_DIM:(kv + 1) * SB_HEAD_DIM])
                             for k in ks for kv in range(SB_KV_HEADS)], axis=0) * SB_SCALE
        soft = jnp.log(1.0 + jnp.exp(-jnp.abs(z)))
        log_keep = -(jnp.maximum(z, 0.0) + soft)
        log_beta = log_keep + z
        if mask is not None:
            log_keep = jnp.where(mask, log_keep, 0.0)
        hi = log_keep.astype(BF16)
        lo = (log_keep - hi.astype(F32)).astype(BF16)
        newer = _dot(jnp.concatenate([hi, lo], axis=0), up_ref[...])
        newer = newer[:n * rows] + newer[n * rows:]
        total = jnp.sum(log_keep, axis=-1, keepdims=True)
        carry = carry_scr[...]
        carries = []
        for b in range(n):
            carries.append(carry)
            carry = carry + total[b * rows:(b + 1) * rows]
        carry_scr[...] = carry
        w = jnp.exp(log_beta + newer + jnp.concatenate(carries, axis=0))
        if mask is not None:
            w = jnp.where(mask, w, 0.0)
        w = w.astype(BF16)
        pv = _dot(w[:rows], vs[0])
        for b in range(1, n):
            pv += _dot(w[b * rows:(b + 1) * rows], vs[b])
        acc_scr[...] += pv

    @pl.when(step == 0)
    def _():
        carry_scr[...] = jnp.zeros(carry_scr.shape, F32)
        acc_scr[...] = jnp.zeros(acc_scr.shape, F32)
        t_q = lax.broadcasted_iota(jnp.int32, (rows, page), 0) & (t_new - 1)
        t_k = lax.broadcasted_iota(jnp.int32, (rows, page), 1)
        attend([_pad_rows(kn_ref[...], page).astype(BF16)], [_pad_rows(vn_ref[...], page).astype(BF16)], t_k < t_q)

    def heads(buf, i):
        return jnp.concatenate([buf[slot, i, pl.ds(kv, page, stride=SB_KV_HEADS), :] for kv in range(SB_KV_HEADS)],
                               axis=1).astype(BF16)

    order = range(pages - 1, -1, -1)
    attend([heads(k_buf, i) for i in order], [heads(v_buf, i) for i in order], None)

    @pl.when(step == steps - 1)
    def _():
        acc = acc_scr[...]
        o_ref[0] = jnp.concatenate([acc[kv * half:(kv + 1) * half, kv * SB_HEAD_DIM:(kv + 1) * SB_HEAD_DIM]
                                    for kv in range(SB_KV_HEADS)], axis=0).astype(BF16)

    @pl.when(g == last)
    def _():
        for c in page_copies(None, 1 - slot):
            c.wait()


def _sb_paged(page_table, q, k_new, v_new, cache_k, cache_v, upper, pages=16):
    batch, rows, _ = q.shape
    t_new = rows // SB_HEADS
    n_pages = page_table.shape[1]
    page = cache_k.shape[1] // SB_KV_HEADS
    steps = n_pages // pages
    kvw = SB_KV_HEADS * SB_HEAD_DIM
    pt = page_table.reshape(-1)

    assert n_pages % pages == 0
    page_rows = page * SB_KV_HEADS
    grid_spec = pltpu.PrefetchScalarGridSpec(
        num_scalar_prefetch=1,
        grid=(batch, steps),
        in_specs=[pl.BlockSpec((1, rows, SB_HEAD_DIM), lambda b, s, pt_ref: (b, 0, 0)),
                  pl.BlockSpec((t_new, kvw), lambda b, s, pt_ref: (b, 0)),
                  pl.BlockSpec((t_new, kvw), lambda b, s, pt_ref: (b, 0)),
                  pl.BlockSpec((page, page), lambda b, s, pt_ref: (0, 0)),
                  pl.BlockSpec(memory_space=pl.ANY),
                  pl.BlockSpec(memory_space=pl.ANY)],
        out_specs=pl.BlockSpec((1, rows, SB_HEAD_DIM), lambda b, s, pt_ref: (b, 0, 0)),
        scratch_shapes=[pltpu.VMEM((2, pages, page_rows, SB_HEAD_DIM), F32),
                        pltpu.VMEM((2, pages, page_rows, SB_HEAD_DIM), F32),
                        pltpu.SemaphoreType.DMA((2, 2)),
                        pltpu.VMEM((rows, 1), F32), pltpu.VMEM((rows, kvw), F32)],
    )
    return pl.pallas_call(
        functools.partial(_sb_paged_kernel, pages=pages, page=page, t_new=t_new),
        grid_spec=grid_spec,
        out_shape=jax.ShapeDtypeStruct((batch, rows, SB_HEAD_DIM), BF16),
        compiler_params=_cparams(("arbitrary", "arbitrary")),
        name="sb_paged",
    )(pt, q, k_new, v_new, upper, cache_k, cache_v)


def _merge_kernel(oa_ref, ob_ref, ga_ref, gb_ref, wpa_ref, wpb_ref, o_ref):
    a = _dot(oa_ref[...], wpa_ref[...])
    b = _dot(ob_ref[...], wpb_ref[...])
    o_ref[...] = (_sigmoid(ga_ref[...]) * a + _sigmoid(gb_ref[...]) * b).astype(BF16)


def _merge(o_a, o_b, proj, wpa, wpb, tm=512, tn=1024):
    n, inner = o_a.shape
    d = wpa.shape[1]
    return pl.pallas_call(
        _merge_kernel,
        grid=(n // tm, d // tn),
        in_specs=[pl.BlockSpec((tm, inner), lambda i, j: (i, 0)),
                  pl.BlockSpec((tm, inner), lambda i, j: (i, 0)),
                  pl.BlockSpec((tm, tn), lambda i, j: (i, COL_GATE_A // tn + j)),
                  pl.BlockSpec((tm, tn), lambda i, j: (i, COL_GATE_B // tn + j)),
                  pl.BlockSpec((inner, tn), lambda i, j: (0, j)),
                  pl.BlockSpec((inner, tn), lambda i, j: (0, j))],
        out_specs=pl.BlockSpec((tm, tn), lambda i, j: (i, j)),
        out_shape=jax.ShapeDtypeStruct((n, d), BF16),
        compiler_params=_cparams(("parallel", "arbitrary")),
        name="gated_merge",
    )(o_a, o_b, proj, proj, wpa, wpb)


def _outproj_kernel(m_ref, x_ref, wo_ref, g_ref, wr_ref, br_ref, x1_ref, h_ref, lg_ref):
    x1 = x_ref[...] + _dot(m_ref[...], wo_ref[...])
    x1_ref[...] = x1
    h = _rms(x1, g_ref[...])
    h_ref[...] = h
    h_hi = h.astype(BF16)
    h_lo = (h - h_hi.astype(F32)).astype(BF16)
    wr = wr_ref[...]
    w_hi = wr.astype(BF16)
    w_lo = (wr - w_hi.astype(F32)).astype(BF16)
    lg_ref[...] = _dot(h_hi, w_hi) + _dot(h_lo, w_hi) + _dot(h_hi, w_lo) + br_ref[...]


def _outproj(merged, x, wo, g_ffn, w_router, b_router, tm=256):
    n, d = x.shape
    rw = w_router.shape[1]
    return pl.pallas_call(
        _outproj_kernel,
        grid=(n // tm,),
        in_specs=[pl.BlockSpec((tm, d), lambda i: (i, 0)),
                  pl.BlockSpec((tm, d), lambda i: (i, 0)),
                  pl.BlockSpec((d, d), lambda i: (0, 0)),
                  pl.BlockSpec((1, d), lambda i: (0, 0)),
                  pl.BlockSpec((d, rw), lambda i: (0, 0)),
                  pl.BlockSpec((1, rw), lambda i: (0, 0))],
        out_specs=[pl.BlockSpec((tm, d), lambda i: (i, 0)),
                   pl.BlockSpec((tm, d), lambda i: (i, 0)),
                   pl.BlockSpec((tm, rw), lambda i: (i, 0))],
        out_shape=[jax.ShapeDtypeStruct((n, d), F32), jax.ShapeDtypeStruct((n, d), F32),
                   jax.ShapeDtypeStruct((n, rw), F32)],
        compiler_params=_cparams(("parallel",)),
        name="out_proj_router",
    )(merged, x, wo, g_ffn, w_router, b_router)


def _route_kernel(lg_ref, id_ref, w_ref):
    lg = lg_ref[...]
    col = lax.broadcasted_iota(jnp.int32, lg.shape, 1)
    big = jnp.int32(1 << 20)

    def first_argmax(v):
        m = jnp.max(v, axis=-1, keepdims=True)
        return m, jnp.min(jnp.where(v == m, col, big), axis=-1, keepdims=True)

    gl = jnp.where(col < N_GROUPS, lg, -jnp.inf)
    g_max, g_idx = first_argmax(gl)
    g_w = 1.0 / jnp.sum(jnp.exp(gl - g_max), axis=-1, keepdims=True)
    lo = N_GROUPS + g_idx * EXPERTS_PER_GROUP
    el = jnp.where(jnp.logical_and(col >= lo, col < lo + EXPERTS_PER_GROUP), lg, -jnp.inf)
    v1, i1 = first_argmax(el)
    v2, i2 = first_argmax(jnp.where(col == i1, -jnp.inf, el))
    e2 = jnp.exp(v2 - v1)
    w1 = 1.0 / (1.0 + e2) * g_w
    w2 = e2 / (1.0 + e2) * g_w
    id_ref[...] = jnp.where(col == 0, i1 - N_GROUPS, jnp.where(col == 1, i2 - N_GROUPS, 0))
    w_ref[...] = jnp.where(col == 0, w1, jnp.where(col == 1, w2, 0.0))


def _route(logits, tm=512):
    n, rw = logits.shape
    spec = pl.BlockSpec((tm, rw), lambda i: (i, 0))
    return pl.pallas_call(
        _route_kernel,
        grid=(n // tm,),
        in_specs=[spec],
        out_specs=[spec, spec],
        out_shape=[jax.ShapeDtypeStruct((n, rw), jnp.int32), jax.ShapeDtypeStruct((n, rw), F32)],
        compiler_params=_cparams(("parallel",)),
        name="route",
    )(logits)


def _moe_kernel(te_ref, nu_ref, x_ref, rw_ref, wg_ref, wu_ref, wd_ref, o_ref):
    tile = pl.program_id(0)

    @pl.when(tile < nu_ref[0])
    def _():
        x = x_ref[...].astype(BF16)
        g = _dot(x, wg_ref[0])
        u = _dot(x, wu_ref[0])
        hid = (g * _sigmoid(g)) * u * rw_ref[...]
        o_ref[...] = _dot(hid.astype(BF16), wd_ref[0])

    @pl.when(tile >= nu_ref[0])
    def _():
        o_ref[...] = jnp.zeros(o_ref.shape, F32)


def _moe(tile_expert, n_used, xs, row_w, wg, wu, wd, tm):
    rows, d = xs.shape
    ff = wg.shape[2]
    grid_spec = pltpu.PrefetchScalarGridSpec(
        num_scalar_prefetch=2,
        grid=(rows // tm,),
        in_specs=[pl.BlockSpec((tm, d), lambda t, te, nu: (t, 0)),
                  pl.BlockSpec((tm, 1), lambda t, te, nu: (t, 0)),
                  pl.BlockSpec((1, d, ff), lambda t, te, nu: (te[t], 0, 0)),
                  pl.BlockSpec((1, d, ff), lambda t, te, nu: (te[t], 0, 0)),
                  pl.BlockSpec((1, ff, d), lambda t, te, nu: (te[t], 0, 0))],
        out_specs=pl.BlockSpec((tm, d), lambda t, te, nu: (t, 0)),
    )
    return pl.pallas_call(
        _moe_kernel,
        grid_spec=grid_spec,
        out_shape=jax.ShapeDtypeStruct((rows, d), F32),
        compiler_params=_cparams(("arbitrary",)),
        name="moe_experts",
    )(tile_expert, n_used, xs, row_w, wg, wu, wd)


def _expert_rows(h, ids, wts, wg, wu, wd, tm=256):
    n = h.shape[0]
    n_assign = n * TOP_K
    n_tiles = n_assign // tm + N_EXPERTS
    eid = ids[:, :TOP_K].reshape(n_assign)
    gate = wts[:, :TOP_K].reshape(n_assign)
    a_idx = jnp.arange(n_assign, dtype=jnp.int32)
    eid_sorted, a_sorted, gate_sorted = lax.sort((eid, a_idx, gate), num_keys=1, is_stable=True)
    counts = jnp.sum(eid[:, None] == jnp.arange(N_EXPERTS)[None, :], axis=0).astype(jnp.int32)
    padded = ((counts + tm - 1) // tm) * tm
    pad_end = jnp.cumsum(padded)
    pad_start = pad_end - padded
    start = jnp.cumsum(counts) - counts
    dest = pad_start[eid_sorted] + (a_idx - start[eid_sorted])
    _, slot = lax.sort((a_sorted, dest), num_keys=1)
    n_used = (pad_end[-1] // tm).astype(jnp.int32)
    tile_start = jnp.minimum(jnp.arange(n_tiles, dtype=jnp.int32), n_used - 1) * tm
    tile_expert = jnp.searchsorted(pad_end, tile_start, side="right").astype(jnp.int32)
    row = jnp.arange(n_tiles * tm, dtype=jnp.int32)
    row_expert = tile_expert[row // tm]
    rank = row - pad_start[row_expert]
    valid = rank < counts[row_expert]
    src = jnp.clip(start[row_expert] + rank, 0, n_assign - 1)
    src_tok = jnp.where(valid, a_sorted[src] // TOP_K, 0)
    row_w = jnp.where(valid, gate_sorted[src], 0.0)
    ys = _moe(tile_expert, n_used[None], h[src_tok], row_w[:, None], wg, wu, wd, tm)
    slot = slot.reshape(n, TOP_K)
    return ys[slot[:, 0]], ys[slot[:, 1]]


def _final_kernel(x_ref, y0_ref, y1_ref, g_ref, o_ref):
    o_ref[...] = _rms(x_ref[...] + (y0_ref[...] + y1_ref[...]), g_ref[...])


def _final(x1, y0, y1, gain, tm=512):
    n, d = x1.shape
    spec = pl.BlockSpec((tm, d), lambda i: (i, 0))
    return pl.pallas_call(
        _final_kernel,
        grid=(n // tm,),
        in_specs=[spec, spec, spec, pl.BlockSpec((1, d), lambda i: (0, 0))],
        out_specs=spec,
        out_shape=jax.ShapeDtypeStruct((n, d), F32),
        compiler_params=_cparams(("parallel",)),
        name="final_norm",
    )(x1, y0, y1, gain)


def _rotate_half_cols(w):
    half = QK_ROPE // 2
    return jnp.concatenate([-w[..., half:], w[..., :half]], axis=-1)


def _prep_w_in(w_in):
    d = w_in.shape[0]
    offs = np.cumsum([0, Q_LORA, KV_LORA, QK_ROPE, SB_HEADS * SB_HEAD_DIM, SB_KV_HEADS * SB_HEAD_DIM,
                      SB_KV_HEADS * SB_HEAD_DIM, d, d])
    c_q, c_kv, k_pe, sb_q, sb_k, sb_v, g_a, g_b = [w_in[:, offs[i]:offs[i + 1]] for i in range(8)]
    parts = [g_a, g_b, sb_q, c_q, c_kv, sb_k, sb_v, k_pe, _rotate_half_cols(k_pe)]
    used = sum(p.shape[1] for p in parts)
    parts.append(jnp.zeros((d, PROJ_WIDTH - used), w_in.dtype))
    return jnp.concatenate(parts, axis=1).astype(BF16)


def _prep_w_uq(w_uq):
    nope = w_uq[:, :, :QK_NOPE].reshape(Q_LORA, MLA_HEADS * QK_NOPE)
    pe = w_uq[:, :, QK_NOPE:]
    pe_rot = _rotate_half_cols(pe)
    return jnp.concatenate([nope, pe.reshape(Q_LORA, -1), pe_rot.reshape(Q_LORA, -1)], axis=1).astype(BF16)


def _rope_tables(positions):
    half = QK_ROPE // 2
    inv = 1.0 / (ROPE_THETA ** (jnp.arange(half, dtype=F32) / half))
    ang = positions.astype(F32)[:, None] * inv[None, :]
    cos, sin = jnp.cos(ang), jnp.sin(ang)
    cos2 = jnp.concatenate([cos, cos], axis=-1)
    sin2 = jnp.concatenate([sin, sin], axis=-1)
    return cos2, sin2


def _strict_upper(n):
    r = np.arange(n)
    return jnp.asarray((r[:, None] > r[None, :]).astype(np.float32), dtype=BF16)


def kernel(x_prompt, x_sample, cache_ckv, cache_kpe, cache_sb_k, cache_sb_v, page_table, norm_mix, w_in, q_norm,
           w_uq, kv_norm, w_uk, w_uv, w_pa, w_pb, w_o, norm_ffn, w_rg, b_rg, w_re, b_re, w_gate, w_up, w_down,
           norm_final):
    depth = w_in.shape[0]
    assert depth == 1
    batch, seq, d = x_prompt.shape
    dec_batch, dec_seq, _ = x_sample.shape
    n_prompt = batch * seq
    n_sample = dec_batch * dec_seq
    n = n_prompt + n_sample
    page = cache_ckv.shape[2]
    past_len = page_table.shape[1] * page
    post_tm = 256
    l = 0

    pos = jnp.concatenate([jnp.arange(seq), past_len + (jnp.arange(post_tm) % dec_seq)])
    cos2, sin2 = _rope_tables(pos)
    cosq = jnp.tile(cos2, (1, MLA_HEADS))
    sinq = jnp.tile(sin2, (1, MLA_HEADS))
    csk = jnp.concatenate([cos2, sin2], axis=1)
    w1 = _prep_w_in(w_in[l])
    wq = _prep_w_uq(w_uq[l])
    wuk_t = jnp.transpose(w_uk[l], (1, 2, 0)).astype(BF16)
    wuv = jnp.transpose(w_uv[l], (1, 0, 2)).astype(BF16)
    rw = 128
    w_router = jnp.concatenate([w_rg[l], w_re[l], jnp.zeros((d, rw - N_GROUPS - N_EXPERTS), F32)], axis=1)
    b_router = jnp.concatenate([b_rg[l], b_re[l], jnp.zeros((rw - N_GROUPS - N_EXPERTS,), F32)])[None, :]

    x = jnp.concatenate([x_prompt.reshape(n_prompt, d), x_sample.reshape(n_sample, d)], axis=0)
    proj = _inproj(x, norm_mix[l][None, :], w1)
    (ckv, kpe, sbk, sbv, ckvb, kpeb, sbkb, sbvb, qlat, qpe, sbq) = _post(
        proj, q_norm[l][None, :], kv_norm[l][None, :], wq, wuk_t, cosq, sinq, csk, n_prompt, seq, tm=post_tm)

    sb_t = 256
    oa_p = _mla_prompt(qlat, qpe, ckvb, kpeb, wuv, batch, seq)
    ob_p = _sb_prompt(sbq, sbkb, sbvb, _strict_upper(sb_t), batch, seq, t=sb_t)

    def per_seq(a):
        a = a[:, n_prompt:].reshape(a.shape[0], dec_batch, dec_seq, a.shape[2])
        return jnp.transpose(a, (1, 0, 2, 3)).reshape(dec_batch, a.shape[0] * dec_seq, a.shape[3])

    n_pool = cache_ckv.shape[1]
    kvw = SB_KV_HEADS * SB_HEAD_DIM
    o_lat_s = _mla_paged(page_table, per_seq(qlat), per_seq(qpe), ckv[n_prompt:], kpe[n_prompt:],
                         cache_ckv[l], jnp.swapaxes(cache_kpe[l], 1, 2))
    o_lat_s = jnp.transpose(o_lat_s.reshape(dec_batch, MLA_HEADS, dec_seq, KV_LORA), (1, 0, 2, 3))
    oa_s = _uv(o_lat_s.reshape(MLA_HEADS, n_sample, KV_LORA), wuv)
    ob_s = _sb_paged(page_table, per_seq(sbq), sbk[n_prompt:], sbv[n_prompt:],
                     cache_sb_k[l].reshape(n_pool, page * SB_KV_HEADS, SB_HEAD_DIM),
                     cache_sb_v[l].reshape(n_pool, page * SB_KV_HEADS, SB_HEAD_DIM),
                     _strict_upper(page))
    ob_s = jnp.transpose(ob_s.reshape(dec_batch, SB_HEADS, dec_seq, SB_HEAD_DIM), (0, 2, 1, 3))
    ob_s = ob_s.reshape(n_sample, SB_HEADS * SB_HEAD_DIM)

    o_a = jnp.concatenate([oa_p, oa_s], axis=0)
    o_b = jnp.concatenate([ob_p, ob_s], axis=0)
    merged = _merge(o_a, o_b, proj, w_pa[l].astype(BF16), w_pb[l].astype(BF16))
    x1, h2, logits = _outproj(merged, x, w_o[l].astype(BF16), norm_ffn[l][None, :], w_router, b_router)
    ids, wts = _route(logits)

    y_first, y_second = _expert_rows(h2, ids, wts, w_gate[l].astype(BF16), w_up[l].astype(BF16),
                                     w_down[l].astype(BF16))
    y = _final(x1, y_first, y_second, norm_final[None, :])

    def rows_p(a, *tail):
        return a[:n_prompt].reshape((depth, batch, seq) + tail)

    def rows_s(a, *tail):
        return a[n_prompt:].reshape((depth, dec_batch, dec_seq) + tail)

    return (y[:n_prompt].reshape(batch, seq, d), y[n_prompt:].reshape(dec_batch, dec_seq, d),
            rows_p(ckv, KV_LORA), rows_p(kpe, QK_ROPE),
            rows_p(sbk, SB_KV_HEADS, SB_HEAD_DIM), rows_p(sbv, SB_KV_HEADS, SB_HEAD_DIM),
            rows_s(ckv, KV_LORA), rows_s(kpe, QK_ROPE),
            rows_s(sbk, SB_KV_HEADS, SB_HEAD_DIM), rows_s(sbv, SB_KV_HEADS, SB_HEAD_DIM))
```

```python
import functools

import jax
import jax.numpy as jnp
import numpy as np
from jax import lax
from jax.experimental import pallas as pl
from jax.experimental.pallas import tpu as pltpu

F32 = jnp.float32
BF16 = jnp.bfloat16

EPS = 1e-6
ROPE_THETA = 10000.0
MLA_HEADS = 8
Q_LORA = 512
KV_LORA = 512
QK_NOPE = 128
QK_ROPE = 64
V_HEAD = 128
MLA_SCALE = (QK_NOPE + QK_ROPE) ** -0.5
SB_HEADS = 8
SB_KV_HEADS = 2
SB_GROUP = SB_HEADS // SB_KV_HEADS
SB_HEAD_DIM = 128
SB_SCALE = SB_HEAD_DIM ** -0.5
N_GROUPS = 4
EXPERTS_PER_GROUP = 8
N_EXPERTS = N_GROUPS * EXPERTS_PER_GROUP
TOP_K = 2
NEG_BIG = -1e30

COL_GATE_A = 0
COL_GATE_B = 2048
COL_SBQ = 4096
COL_CQ = 5120
COL_CKV = 5632
COL_SBKV = 6144
COL_KPE = 6656
PROJ_WIDTH = 7168

VMEM_LIMIT = 48 * 1024 * 1024


def _cparams(semantics):
    return pltpu.CompilerParams(dimension_semantics=semantics, vmem_limit_bytes=VMEM_LIMIT)


def _dot(a, b):
    return jnp.dot(a, b, preferred_element_type=F32)


def _dot_nt(a, b):
    return lax.dot_general(a, b, (((1,), (1,)), ((), ())), preferred_element_type=F32)


def _rms(x, gain):
    return x * lax.rsqrt(jnp.mean(x * x, axis=-1, keepdims=True) + EPS) * gain


def _sigmoid(x):
    return 1.0 / (1.0 + jnp.exp(-x))


def _inproj_kernel(x_ref, g_ref, w_ref, o_ref, h_scr):
    @pl.when(pl.program_id(1) == 0)
    def _():
        h_scr[...] = _rms(x_ref[...], g_ref[...]).astype(BF16)

    o_ref[...] = _dot(h_scr[...], w_ref[...])


def _inproj(x, gain, w1, tm=512, tn=1024):
    n, d = x.shape
    width = w1.shape[1]
    return pl.pallas_call(
        _inproj_kernel,
        grid=(n // tm, width // tn),
        in_specs=[pl.BlockSpec((tm, d), lambda i, j: (i, 0)),
                  pl.BlockSpec((1, d), lambda i, j: (0, 0)),
                  pl.BlockSpec((d, tn), lambda i, j: (0, j))],
        out_specs=pl.BlockSpec((tm, tn), lambda i, j: (i, j)),
        out_shape=jax.ShapeDtypeStruct((n, width), F32),
        scratch_shapes=[pltpu.VMEM((tm, d), BF16)],
        compiler_params=_cparams(("parallel", "arbitrary")),
        name="inproj",
    )(x, gain, w1)


def _post_kernel(cq_ref, ckv_ref, sbkv_ref, kpe_ref, sbq_ref, qn_ref, kvn_ref, wq_ref, wuk_ref,
                 cosq_ref, sinq_ref, csk_ref,
                 ckv_o, kpe_o, sbk_o, sbv_o, ckvb_o, kpeb_o, sbkb_o, sbvb_o, qlat_o, qpe_o, sbqb_o):
    cqn = _rms(cq_ref[...], qn_ref[...]).astype(BF16)
    q = _dot(cqn, wq_ref[...])
    for h in range(MLA_HEADS):
        q_nope = q[:, QK_NOPE * h:QK_NOPE * (h + 1)].astype(BF16)
        qlat_o[h] = _dot(q_nope, wuk_ref[h]).astype(BF16)
    pe0 = MLA_HEADS * QK_NOPE
    pe1 = pe0 + MLA_HEADS * QK_ROPE
    roped = q[:, pe0:pe1] * cosq_ref[...] + q[:, pe1:pe1 + MLA_HEADS * QK_ROPE] * sinq_ref[...]
    for h in range(MLA_HEADS):
        qpe_o[h] = roped[:, QK_ROPE * h:QK_ROPE * (h + 1)].astype(BF16)

    ckv = _rms(ckv_ref[...], kvn_ref[...])
    ckv_o[...] = ckv
    ckvb_o[...] = ckv.astype(BF16)

    prod = kpe_ref[...] * csk_ref[...]
    kpe = prod[:, :QK_ROPE] + prod[:, QK_ROPE:]
    kpe_o[...] = kpe
    kpeb_o[...] = kpe.astype(BF16)

    kvw = SB_KV_HEADS * SB_HEAD_DIM
    sbk = sbkv_ref[:, :kvw]
    sbv = sbkv_ref[:, kvw:]
    sbk_o[...] = sbk
    sbv_o[...] = sbv
    sbkb_o[...] = sbk.astype(BF16)
    sbvb_o[...] = sbv.astype(BF16)
    for h in range(SB_HEADS):
        sbqb_o[h] = sbq_ref[:, SB_HEAD_DIM * h:SB_HEAD_DIM * (h + 1)].astype(BF16)


def _post(proj, q_norm, kv_norm, wq, wuk_t, cosq, sinq, csk, n_prompt, seq, tm=256):
    n = proj.shape[0]
    prompt_blocks = n_prompt // tm
    table_blocks = seq // tm

    def tab(i):
        return (jnp.where(i < prompt_blocks, i % table_blocks, table_blocks), 0)

    def col(width, offset):
        return pl.BlockSpec((tm, width), lambda i: (i, offset // width))

    def full(shape):
        return pl.BlockSpec(shape, lambda i: (0,) * len(shape))

    def rows(width, dtype):
        return pl.BlockSpec((tm, width), lambda i: (i, 0)), jax.ShapeDtypeStruct((n, width), dtype)

    def heads(width):
        return (pl.BlockSpec((MLA_HEADS, tm, width), lambda i: (0, i, 0)),
                jax.ShapeDtypeStruct((MLA_HEADS, n, width), BF16))

    kvw = SB_KV_HEADS * SB_HEAD_DIM
    outs = [rows(KV_LORA, F32), rows(QK_ROPE, F32), rows(kvw, F32), rows(kvw, F32),
            rows(KV_LORA, BF16), rows(QK_ROPE, BF16), rows(kvw, BF16), rows(kvw, BF16),
            heads(KV_LORA), heads(QK_ROPE), heads(SB_HEAD_DIM)]
    return pl.pallas_call(
        _post_kernel,
        grid=(n // tm,),
        in_specs=[col(Q_LORA, COL_CQ), col(KV_LORA, COL_CKV), col(2 * kvw, COL_SBKV),
                  col(2 * QK_ROPE, COL_KPE), col(SB_HEADS * SB_HEAD_DIM, COL_SBQ),
                  full((1, Q_LORA)), full((1, KV_LORA)), full(wq.shape), full(wuk_t.shape),
                  pl.BlockSpec((tm, MLA_HEADS * QK_ROPE), tab),
                  pl.BlockSpec((tm, MLA_HEADS * QK_ROPE), tab),
                  pl.BlockSpec((tm, 2 * QK_ROPE), tab)],
        out_specs=[o[0] for o in outs],
        out_shape=[o[1] for o in outs],
        compiler_params=_cparams(("parallel",)),
        name="post_proj",
    )(proj, proj, proj, proj, proj, q_norm, kv_norm, wq, wuk_t, cosq, sinq, csk)


def _mla_prompt_kernel(qlat_ref, qpe_ref, ckv_ref, kpe_ref, wuv_ref, o_ref, m_scr, l_scr, acc_scr, *, tq, tk):
    qi = pl.program_id(1)
    ki = pl.program_id(2)
    rows = MLA_HEADS * tq
    last = ((qi + 1) * tq - 1) // tk

    @pl.when(ki == 0)
    def _():
        m_scr[...] = jnp.full(m_scr.shape, NEG_BIG, F32)
        l_scr[...] = jnp.zeros(l_scr.shape, F32)
        acc_scr[...] = jnp.zeros(acc_scr.shape, F32)

    @pl.when(ki <= last)
    def _():
        q = qlat_ref[...].reshape(rows, KV_LORA)
        qp = qpe_ref[...].reshape(rows, QK_ROPE)
        k = ckv_ref[...]
        s = (_dot_nt(q, k) + _dot_nt(qp, kpe_ref[...])) * MLA_SCALE
        q_pos = qi * tq + (lax.broadcasted_iota(jnp.int32, (rows, tk), 0) & (tq - 1))
        k_pos = ki * tk + lax.broadcasted_iota(jnp.int32, (rows, tk), 1)
        s = jnp.where(k_pos <= q_pos, s, -jnp.inf)
        m_prev = m_scr[...]
        m_new = jnp.maximum(m_prev, jnp.max(s, axis=-1, keepdims=True))
        alpha = jnp.exp(m_prev - m_new)
        p = jnp.exp(s - m_new)
        l_scr[...] = alpha * l_scr[...] + jnp.sum(p, axis=-1, keepdims=True)
        acc_scr[...] = alpha * acc_scr[...] + _dot(p.astype(BF16), k)
        m_scr[...] = m_new

    @pl.when(ki == pl.num_programs(2) - 1)
    def _():
        o_lat = (acc_scr[...] / l_scr[...]).astype(BF16)
        for h in range(MLA_HEADS):
            o_ref[:, V_HEAD * h:V_HEAD * (h + 1)] = _dot(o_lat[h * tq:(h + 1) * tq], wuv_ref[h]).astype(BF16)


def _mla_prompt(qlat, qpe, ckvb, kpeb, wuv, batch, seq, tq=128, tk=512):
    nq = seq // tq
    nk = seq // tk
    rows = MLA_HEADS * tq

    def kmap(b, qi, ki):
        return (b * nk + jnp.minimum(ki, ((qi + 1) * tq - 1) // tk), 0)

    return pl.pallas_call(
        functools.partial(_mla_prompt_kernel, tq=tq, tk=tk),
        grid=(batch, nq, nk),
        in_specs=[pl.BlockSpec((MLA_HEADS, tq, KV_LORA), lambda b, qi, ki: (0, b * nq + qi, 0)),
                  pl.BlockSpec((MLA_HEADS, tq, QK_ROPE), lambda b, qi, ki: (0, b * nq + qi, 0)),
                  pl.BlockSpec((tk, KV_LORA), kmap),
                  pl.BlockSpec((tk, QK_ROPE), kmap),
                  pl.BlockSpec(wuv.shape, lambda b, qi, ki: (0, 0, 0))],
        out_specs=pl.BlockSpec((tq, MLA_HEADS * V_HEAD), lambda b, qi, ki: (b * nq + qi, 0)),
        out_shape=jax.ShapeDtypeStruct((batch * seq, MLA_HEADS * V_HEAD), BF16),
        scratch_shapes=[pltpu.VMEM((rows, 1), F32), pltpu.VMEM((rows, 1), F32), pltpu.VMEM((rows, KV_LORA), F32)],
        compiler_params=_cparams(("parallel", "parallel", "arbitrary")),
        name="mla_prompt",
    )(qlat, qpe, ckvb, kpeb, wuv)


def _sb_block(z, mask, upper, carry):
    soft = jnp.log(1.0 + jnp.exp(-jnp.abs(z)))
    log_keep = -(jnp.maximum(z, 0.0) + soft)
    log_beta = log_keep + z
    if mask is not None:
        log_keep = jnp.where(mask, log_keep, 0.0)
    hi = log_keep.astype(BF16)
    lo = (log_keep - hi.astype(F32)).astype(BF16)
    newer = _dot(hi, upper) + _dot(lo, upper)
    w = jnp.exp(log_beta + newer + carry)
    if mask is not None:
        w = jnp.where(mask, w, 0.0)
    return w, jnp.sum(log_keep, axis=-1, keepdims=True)


def _sb_prompt_kernel(q_ref, k_ref, v_ref, up_ref, o_ref, carry_scr, acc_scr, *, t):
    qi = pl.program_id(2)
    step = pl.program_id(3)
    rows = SB_GROUP * t

    @pl.when(step == 0)
    def _():
        carry_scr[...] = jnp.zeros(carry_scr.shape, F32)
        acc_scr[...] = jnp.zeros(acc_scr.shape, F32)

    def block(masked):
        q = q_ref[...].reshape(rows, SB_HEAD_DIM)
        z = _dot_nt(q, k_ref[...]) * SB_SCALE
        mask = None
        if masked:
            q_pos = lax.broadcasted_iota(jnp.int32, (rows, t), 0) & (t - 1)
            k_pos = lax.broadcasted_iota(jnp.int32, (rows, t), 1)
            mask = k_pos < q_pos
        w, total = _sb_block(z, mask, up_ref[...], carry_scr[...])
        acc_scr[...] += _dot(w.astype(BF16), v_ref[...])
        carry_scr[...] += total

    @pl.when(step == 0)
    def _():
        block(True)

    @pl.when(jnp.logical_and(step > 0, step <= qi))
    def _():
        block(False)

    @pl.when(step == pl.num_programs(3) - 1)
    def _():
        for g in range(SB_GROUP):
            o_ref[:, SB_HEAD_DIM * g:SB_HEAD_DIM * (g + 1)] = acc_scr[g * t:(g + 1) * t].astype(BF16)


def _sb_prompt(sbq, sbkb, sbvb, upper, batch, seq, t=256):
    nq = seq // t
    rows = SB_GROUP * t

    def kvmap(b, kv, qi, step):
        return (b * nq + jnp.maximum(qi - step, 0), kv)

    return pl.pallas_call(
        functools.partial(_sb_prompt_kernel, t=t),
        grid=(batch, SB_KV_HEADS, nq, nq),
        in_specs=[pl.BlockSpec((SB_GROUP, t, SB_HEAD_DIM), lambda b, kv, qi, step: (kv, b * nq + qi, 0)),
                  pl.BlockSpec((t, SB_HEAD_DIM), kvmap),
                  pl.BlockSpec((t, SB_HEAD_DIM), kvmap),
                  pl.BlockSpec((t, t), lambda b, kv, qi, step: (0, 0))],
        out_specs=pl.BlockSpec((t, SB_GROUP * SB_HEAD_DIM), lambda b, kv, qi, step: (b * nq + qi, kv)),
        out_shape=jax.ShapeDtypeStruct((batch * seq, SB_HEADS * SB_HEAD_DIM), BF16),
        scratch_shapes=[pltpu.VMEM((rows, 1), F32), pltpu.VMEM((rows, SB_HEAD_DIM), F32)],
        compiler_params=_cparams(("parallel", "parallel", "parallel", "arbitrary")),
        name="sb_prompt",
    )(sbq, sbkb, sbvb, upper)


def _pad_rows(x, rows):
    return jnp.concatenate([x, jnp.zeros((rows - x.shape[0], x.shape[1]), x.dtype)], axis=0)


def _mla_paged_kernel(pt_ref, q_ref, qpe_ref, w2_ref, ckvn_ref, kpen_ref, ckv_hbm, kpe_hbm, o_ref,
                      ckv_buf, kpe_buf, sem, m_scr, l_scr, acc_scr, *, pages, page, t_new):
    step = pl.program_id(1)
    steps = pl.num_programs(1)
    g = pl.program_id(0) * steps + step
    last = pl.num_programs(0) * steps - 1
    slot = lax.rem(g, 2)
    rows = MLA_HEADS * t_new
    qp = qpe_ref[0]

    def page_copies(g_idx, slot_idx):
        out = []
        for i in range(pages):
            pid = 0 if g_idx is None else pt_ref[g_idx * pages + i]
            out.append(pltpu.make_async_copy(ckv_hbm.at[pid], ckv_buf.at[slot_idx, i], sem.at[slot_idx, 0]))
            out.append(pltpu.make_async_copy(kpe_hbm.at[pid], kpe_buf.at[slot_idx, i], sem.at[slot_idx, 1]))
        return out

    @pl.when(g == 0)
    def _():
        for i, c in enumerate(page_copies(0, 0)):
            c.start(priority=(i // 2) % 2)

    for c in page_copies(None, slot):
        c.wait()
    for i, c in enumerate(page_copies(jnp.minimum(g + 1, last), 1 - slot)):
        c.start(priority=(i // 2) % 2)

    def update(s, kcat):
        width = s.shape[1] // len(kcat)
        m_prev = m_scr[...]
        m_new = jnp.maximum(m_prev, jnp.max(s, axis=-1, keepdims=True))
        alpha = jnp.exp(m_prev - m_new)
        p = jnp.exp(s - m_new)
        l_scr[...] = alpha * l_scr[...] + jnp.sum(p, axis=-1, keepdims=True)
        p = p.astype(BF16)
        pv = _dot(p[:, :width], kcat[0])
        for i in range(1, len(kcat)):
            pv += _dot(p[:, i * width:(i + 1) * width], kcat[i])
        acc_scr[...] = alpha * acc_scr[...] + pv
        m_scr[...] = m_new

    @pl.when(step == 0)
    def _():
        m_scr[...] = jnp.full(m_scr.shape, NEG_BIG, F32)
        l_scr[...] = jnp.zeros(l_scr.shape, F32)
        acc_scr[...] = jnp.zeros(acc_scr.shape, F32)
        k = _pad_rows(ckvn_ref[...], page).astype(BF16)
        kp = _pad_rows(kpen_ref[...], page).astype(BF16)
        s = (_dot_nt(q_ref[0], k) + _dot_nt(qp, kp)) * MLA_SCALE
        t_q = lax.broadcasted_iota(jnp.int32, (rows, page), 0) & (t_new - 1)
        t_k = lax.broadcasted_iota(jnp.int32, (rows, page), 1)
        update(jnp.where(t_k <= t_q, s, -jnp.inf), [k])

    ks = [ckv_buf[slot, i].astype(BF16) for i in range(pages)]
    kps = [kpe_buf[slot, i].astype(BF16) for i in range(pages)]
    pairs = pages // 2
    lhs = jnp.concatenate([jnp.concatenate([ks[2 * j], ks[2 * j + 1]], axis=1) for j in range(pairs)], axis=0)
    st = _dot(lhs, w2_ref[0])
    nope = []
    for j in range(pairs):
        both = st[j * page:(j + 1) * page].T
        nope += [both[:rows], both[rows:]]
    s = jnp.concatenate([nope[i] + _dot(qp, kps[i]) for i in range(pages)], axis=1) * MLA_SCALE
    update(s, [jnp.concatenate([ks[2 * j], ks[2 * j + 1]], axis=0) for j in range(pairs)])

    @pl.when(step == steps - 1)
    def _():
        o_ref[0] = (acc_scr[...] / l_scr[...]).astype(BF16)

    @pl.when(g == last)
    def _():
        for c in page_copies(None, 1 - slot):
            c.wait()


def _mla_paged(page_table, q, qpe, ckv_new, kpe_new, cache_ckv, cache_kpe_t, pages=16):
    batch, rows, _ = q.shape
    t_new = rows // MLA_HEADS
    n_pages = page_table.shape[1]
    page = cache_ckv.shape[1]
    assert 2 * rows == page and n_pages % pages == 0 and pages % 2 == 0
    steps = n_pages // pages
    pt = page_table.reshape(-1)
    q_t = jnp.swapaxes(q, 1, 2)
    zero = jnp.zeros_like(q_t)
    w2 = jnp.concatenate([jnp.concatenate([q_t, zero], axis=2), jnp.concatenate([zero, q_t], axis=2)], axis=1)

    grid_spec = pltpu.PrefetchScalarGridSpec(
        num_scalar_prefetch=1,
        grid=(batch, steps),
        in_specs=[pl.BlockSpec((1, rows, KV_LORA), lambda b, s, pt_ref: (b, 0, 0)),
                  pl.BlockSpec((1, rows, QK_ROPE), lambda b, s, pt_ref: (b, 0, 0)),
                  pl.BlockSpec((1, 2 * KV_LORA, 2 * rows), lambda b, s, pt_ref: (b, 0, 0)),
                  pl.BlockSpec((t_new, KV_LORA), lambda b, s, pt_ref: (b, 0)),
                  pl.BlockSpec((t_new, QK_ROPE), lambda b, s, pt_ref: (b, 0)),
                  pl.BlockSpec(memory_space=pl.ANY),
                  pl.BlockSpec(memory_space=pl.ANY)],
        out_specs=pl.BlockSpec((1, rows, KV_LORA), lambda b, s, pt_ref: (b, 0, 0)),
        scratch_shapes=[pltpu.VMEM((2, pages, page, KV_LORA), F32), pltpu.VMEM((2, pages, QK_ROPE, page), F32),
                        pltpu.SemaphoreType.DMA((2, 2)),
                        pltpu.VMEM((rows, 1), F32), pltpu.VMEM((rows, 1), F32), pltpu.VMEM((rows, KV_LORA), F32)],
    )
    return pl.pallas_call(
        functools.partial(_mla_paged_kernel, pages=pages, page=page, t_new=t_new),
        grid_spec=grid_spec,
        out_shape=jax.ShapeDtypeStruct((batch, rows, KV_LORA), BF16),
        compiler_params=_cparams(("arbitrary", "arbitrary")),
        name="mla_paged",
    )(pt, q, qpe, w2, ckv_new, kpe_new, cache_ckv, cache_kpe_t)


def _uv_kernel(o_ref, w_ref, out_ref):
    for h in range(MLA_HEADS):
        out_ref[:, V_HEAD * h:V_HEAD * (h + 1)] = _dot(o_ref[h], w_ref[h]).astype(BF16)


def _uv(o_lat, wuv):
    n = o_lat.shape[1]
    return pl.pallas_call(
        _uv_kernel,
        grid=(1,),
        in_specs=[pl.BlockSpec(o_lat.shape, lambda i: (0, 0, 0)), pl.BlockSpec(wuv.shape, lambda i: (0, 0, 0))],
        out_specs=pl.BlockSpec((n, MLA_HEADS * V_HEAD), lambda i: (0, 0)),
        out_shape=jax.ShapeDtypeStruct((n, MLA_HEADS * V_HEAD), BF16),
        compiler_params=_cparams(("arbitrary",)),
        name="mla_value_up",
    )(o_lat, wuv)


def _sb_paged_kernel(pt_ref, q_ref, kn_ref, vn_ref, up_ref, k_hbm, v_hbm, o_ref,
                     k_buf, v_buf, sem, carry_scr, acc_scr, *, pages, page, t_new):
    step = pl.program_id(1)
    steps = pl.num_programs(1)
    g = pl.program_id(0) * steps + step
    last = pl.num_programs(0) * steps - 1
    slot = lax.rem(g, 2)
    half = SB_GROUP * t_new
    rows = SB_KV_HEADS * half
    q = q_ref[0]

    def page_copies(g_idx, slot_idx):
        base = 0
        if g_idx is not None:
            base = (lax.div(g_idx, steps) * steps + (steps - 1 - lax.rem(g_idx, steps))) * pages
        out = []
        for i in range(pages):
            pid = 0 if g_idx is None else pt_ref[base + i]
            out.append(pltpu.make_async_copy(k_hbm.at[pid], k_buf.at[slot_idx, i], sem.at[slot_idx, 0]))
            out.append(pltpu.make_async_copy(v_hbm.at[pid], v_buf.at[slot_idx, i], sem.at[slot_idx, 1]))
        return out

    @pl.when(g == 0)
    def _():
        for i, c in enumerate(page_copies(0, 0)):
            c.start(priority=(i // 2) % 2)

    for c in page_copies(None, slot):
        c.wait()
    for i, c in enumerate(page_copies(jnp.minimum(g + 1, last), 1 - slot)):
        c.start(priority=(i // 2) % 2)

    def attend(ks, vs, mask):
        n = len(ks)
        z = jnp.concatenate([_dot_nt(q[kv * half:(kv + 1) * half], k[:, kv * SB_HEAD_DIM:(kv + 1) * SB_HEAD_DIM])
                             for k in ks for kv in range(SB_KV_HEADS)], axis=0) * SB_SCALE
        soft = jnp.log(1.0 + jnp.exp(-jnp.abs(z)))
        log_keep = -(jnp.maximum(z, 0.0) + soft)
        log_beta = log_keep + z
        if mask is not None:
            log_keep = jnp.where(mask, log_keep, 0.0)
        hi = log_keep.astype(BF16)
        lo = (log_keep - hi.astype(F32)).astype(BF16)
        newer = _dot(jnp.concatenate([hi, lo], axis=0), up_ref[...])
        newer = newer[:n * rows] + newer[n * rows:]
        total = jnp.sum(log_keep, axis=-1, keepdims=True)
        carry = carry_scr[...]
        carries = []
        for b in range(n):
            carries.append(carry)
            carry = carry + total[b * rows:(b + 1) * rows]
        carry_scr[...] = carry
        w = jnp.exp(log_beta + newer + jnp.concatenate(carries, axis=0))
        if mask is not None:
            w = jnp.where(mask, w, 0.0)
        w = w.astype(BF16)
        pv = _dot(w[:rows], vs[0])
        for b in range(1, n):
            pv += _dot(w[b * rows:(b + 1) * rows], vs[b])
        acc_scr[...] += pv

    @pl.when(step == 0)
    def _():
        carry_scr[...] = jnp.zeros(carry_scr.shape, F32)
        acc_scr[...] = jnp.zeros(acc_scr.shape, F32)
        t_q = lax.broadcasted_iota(jnp.int32, (rows, page), 0) & (t_new - 1)
        t_k = lax.broadcasted_iota(jnp.int32, (rows, page), 1)
        attend([_pad_rows(kn_ref[...], page).astype(BF16)], [_pad_rows(vn_ref[...], page).astype(BF16)], t_k < t_q)

    def heads(buf, i):
        return jnp.concatenate([buf[slot, i, pl.ds(kv, page, stride=SB_KV_HEADS), :] for kv in range(SB_KV_HEADS)],
                               axis=1).astype(BF16)

    order = range(pages - 1, -1, -1)
    attend([heads(k_buf, i) for i in order], [heads(v_buf, i) for i in order], None)

    @pl.when(step == steps - 1)
    def _():
        acc = acc_scr[...]
        o_ref[0] = jnp.concatenate([acc[kv * half:(kv + 1) * half, kv * SB_HEAD_DIM:(kv + 1) * SB_HEAD_DIM]
                                    for kv in range(SB_KV_HEADS)], axis=0).astype(BF16)

    @pl.when(g == last)
    def _():
        for c in page_copies(None, 1 - slot):
            c.wait()


def _sb_paged(page_table, q, k_new, v_new, cache_k, cache_v, upper, pages=16):
    batch, rows, _ = q.shape
    t_new = rows // SB_HEADS
    n_pages = page_table.shape[1]
    page = cache_k.shape[1] // SB_KV_HEADS
    steps = n_pages // pages
    kvw = SB_KV_HEADS * SB_HEAD_DIM
    pt = page_table.reshape(-1)

    assert n_pages % pages == 0
    page_rows = page * SB_KV_HEADS
    grid_spec = pltpu.PrefetchScalarGridSpec(
        num_scalar_prefetch=1,
        grid=(batch, steps),
        in_specs=[pl.BlockSpec((1, rows, SB_HEAD_DIM), lambda b, s, pt_ref: (b, 0, 0)),
                  pl.BlockSpec((t_new, kvw), lambda b, s, pt_ref: (b, 0)),
                  pl.BlockSpec((t_new, kvw), lambda b, s, pt_ref: (b, 0)),
                  pl.BlockSpec((page, page), lambda b, s, pt_ref: (0, 0)),
                  pl.BlockSpec(memory_space=pl.ANY),
                  pl.BlockSpec(memory_space=pl.ANY)],
        out_specs=pl.BlockSpec((1, rows, SB_HEAD_DIM), lambda b, s, pt_ref: (b, 0, 0)),
        scratch_shapes=[pltpu.VMEM((2, pages, page_rows, SB_HEAD_DIM), F32),
                        pltpu.VMEM((2, pages, page_rows, SB_HEAD_DIM), F32),
                        pltpu.SemaphoreType.DMA((2, 2)),
                        pltpu.VMEM((rows, 1), F32), pltpu.VMEM((rows, kvw), F32)],
    )
    return pl.pallas_call(
        functools.partial(_sb_paged_kernel, pages=pages, page=page, t_new=t_new),
        grid_spec=grid_spec,
        out_shape=jax.ShapeDtypeStruct((batch, rows, SB_HEAD_DIM), BF16),
        compiler_params=_cparams(("arbitrary", "arbitrary")),
        name="sb_paged",
    )(pt, q, k_new, v_new, upper, cache_k, cache_v)


def _merge_kernel(oa_ref, ob_ref, ga_ref, gb_ref, wpa_ref, wpb_ref, o_ref):
    a = _dot(oa_ref[...], wpa_ref[...])
    b = _dot(ob_ref[...], wpb_ref[...])
    o_ref[...] = (_sigmoid(ga_ref[...]) * a + _sigmoid(gb_ref[...]) * b).astype(BF16)


def _merge(o_a, o_b, proj, wpa, wpb, tm=512, tn=1024):
    n, inner = o_a.shape
    d = wpa.shape[1]
    return pl.pallas_call(
        _merge_kernel,
        grid=(n // tm, d // tn),
        in_specs=[pl.BlockSpec((tm, inner), lambda i, j: (i, 0)),
                  pl.BlockSpec((tm, inner), lambda i, j: (i, 0)),
                  pl.BlockSpec((tm, tn), lambda i, j: (i, COL_GATE_A // tn + j)),
                  pl.BlockSpec((tm, tn), lambda i, j: (i, COL_GATE_B // tn + j)),
                  pl.BlockSpec((inner, tn), lambda i, j: (0, j)),
                  pl.BlockSpec((inner, tn), lambda i, j: (0, j))],
        out_specs=pl.BlockSpec((tm, tn), lambda i, j: (i, j)),
        out_shape=jax.ShapeDtypeStruct((n, d), BF16),
        compiler_params=_cparams(("parallel", "arbitrary")),
        name="gated_merge",
    )(o_a, o_b, proj, proj, wpa, wpb)


def _outproj_kernel(m_ref, x_ref, wo_ref, g_ref, wr_ref, br_ref, x1_ref, h_ref, lg_ref):
    x1 = x_ref[...] + _dot(m_ref[...], wo_ref[...])
    x1_ref[...] = x1
    h = _rms(x1, g_ref[...])
    h_ref[...] = h
    h_hi = h.astype(BF16)
    h_lo = (h - h_hi.astype(F32)).astype(BF16)
    wr = wr_ref[...]
    w_hi = wr.astype(BF16)
    w_lo = (wr - w_hi.astype(F32)).astype(BF16)
    lg_ref[...] = _dot(h_hi, w_hi) + _dot(h_lo, w_hi) + _dot(h_hi, w_lo) + br_ref[...]


def _outproj(merged, x, wo, g_ffn, w_router, b_router, tm=256):
    n, d = x.shape
    rw = w_router.shape[1]
    return pl.pallas_call(
        _outproj_kernel,
        grid=(n // tm,),
        in_specs=[pl.BlockSpec((tm, d), lambda i: (i, 0)),
                  pl.BlockSpec((tm, d), lambda i: (i, 0)),
                  pl.BlockSpec((d, d), lambda i: (0, 0)),
                  pl.BlockSpec((1, d), lambda i: (0, 0)),
                  pl.BlockSpec((d, rw), lambda i: (0, 0)),
                  pl.BlockSpec((1, rw), lambda i: (0, 0))],
        out_specs=[pl.BlockSpec((tm, d), lambda i: (i, 0)),
                   pl.BlockSpec((tm, d), lambda i: (i, 0)),
                   pl.BlockSpec((tm, rw), lambda i: (i, 0))],
        out_shape=[jax.ShapeDtypeStruct((n, d), F32), jax.ShapeDtypeStruct((n, d), F32),
                   jax.ShapeDtypeStruct((n, rw), F32)],
        compiler_params=_cparams(("parallel",)),
        name="out_proj_router",
    )(merged, x, wo, g_ffn, w_router, b_router)


def _route_kernel(lg_ref, id_ref, w_ref):
    lg = lg_ref[...]
    col = lax.broadcasted_iota(jnp.int32, lg.shape, 1)
    big = jnp.int32(1 << 20)

    def first_argmax(v):
        m = jnp.max(v, axis=-1, keepdims=True)
        return m, jnp.min(jnp.where(v == m, col, big), axis=-1, keepdims=True)

    gl = jnp.where(col < N_GROUPS, lg, -jnp.inf)
    g_max, g_idx = first_argmax(gl)
    g_w = 1.0 / jnp.sum(jnp.exp(gl - g_max), axis=-1, keepdims=True)
    lo = N_GROUPS + g_idx * EXPERTS_PER_GROUP
    el = jnp.where(jnp.logical_and(col >= lo, col < lo + EXPERTS_PER_GROUP), lg, -jnp.inf)
    v1, i1 = first_argmax(el)
    v2, i2 = first_argmax(jnp.where(col == i1, -jnp.inf, el))
    e2 = jnp.exp(v2 - v1)
    w1 = 1.0 / (1.0 + e2) * g_w
    w2 = e2 / (1.0 + e2) * g_w
    id_ref[...] = jnp.where(col == 0, i1 - N_GROUPS, jnp.where(col == 1, i2 - N_GROUPS, 0))
    w_ref[...] = jnp.where(col == 0, w1, jnp.where(col == 1, w2, 0.0))


def _route(logits, tm=512):
    n, rw = logits.shape
    spec = pl.BlockSpec((tm, rw), lambda i: (i, 0))
    return pl.pallas_call(
        _route_kernel,
        grid=(n // tm,),
        in_specs=[spec],
        out_specs=[spec, spec],
        out_shape=[jax.ShapeDtypeStruct((n, rw), jnp.int32), jax.ShapeDtypeStruct((n, rw), F32)],
        compiler_params=_cparams(("parallel",)),
        name="route",
    )(logits)


def _moe_kernel(te_ref, nu_ref, src_ref, dst_ref, rw_ref, wg_ref, wu_ref, wd_ref, h_hbm, y_hbm,
                x_buf, o_buf, sem_in, sem_out, *, tm):
    tile = pl.program_id(0)
    last = pl.num_programs(0) - 1
    slot = lax.rem(tile, 2)

    def row_reads(tile_idx, slot_idx):
        out = []
        for r in range(tm):
            src = 0 if tile_idx is None else src_ref[tile_idx * tm + r]
            out.append(pltpu.make_async_copy(h_hbm.at[pl.ds(src, 1)], x_buf.at[slot_idx, pl.ds(r, 1)],
                                             sem_in.at[slot_idx]))
        return out

    def row_writes(tile_idx, slot_idx):
        out = []
        for r in range(tm):
            dst = 0 if tile_idx is None else dst_ref[tile_idx * tm + r]
            out.append(pltpu.make_async_copy(o_buf.at[slot_idx, pl.ds(r, 1)], y_hbm.at[pl.ds(dst, 1)],
                                             sem_out.at[slot_idx]))
        return out

    @pl.when(tile == 0)
    def _():
        for r, c in enumerate(row_reads(0, 0)):
            c.start(priority=r % 2)

    for c in row_reads(None, slot):
        c.wait()
    for r, c in enumerate(row_reads(jnp.minimum(tile + 1, last), 1 - slot)):
        c.start(priority=r % 2)

    @pl.when(tile < nu_ref[0])
    def _():
        x = x_buf[slot].astype(BF16)
        g = _dot(x, wg_ref[0].astype(BF16))
        u = _dot(x, wu_ref[0].astype(BF16))
        hid = (g * _sigmoid(g)) * u * rw_ref[...]
        o_buf[slot] = _dot(hid.astype(BF16), wd_ref[0].astype(BF16))

    @pl.when(tile >= nu_ref[0])
    def _():
        o_buf[slot] = jnp.zeros(o_buf.shape[1:], F32)

    @pl.when(tile > 0)
    def _():
        for c in row_writes(None, 1 - slot):
            c.wait()

    for r, c in enumerate(row_writes(tile, slot)):
        c.start(priority=r % 2)

    @pl.when(tile == last)
    def _():
        for c in row_writes(None, slot):
            c.wait()
        for c in row_reads(None, 1 - slot):
            c.wait()


def _moe(tile_expert, n_used, src_tok, dst_row, h, row_w, wg, wu, wd, out_rows, tm):
    rows = src_tok.shape[0]
    d = h.shape[1]
    ff = wg.shape[2]
    grid_spec = pltpu.PrefetchScalarGridSpec(
        num_scalar_prefetch=4,
        grid=(rows // tm,),
        in_specs=[pl.BlockSpec((tm, 1), lambda t, te, nu, src, dst: (t, 0)),
                  pl.BlockSpec((1, d, ff), lambda t, te, nu, src, dst: (te[t], 0, 0)),
                  pl.BlockSpec((1, d, ff), lambda t, te, nu, src, dst: (te[t], 0, 0)),
                  pl.BlockSpec((1, ff, d), lambda t, te, nu, src, dst: (te[t], 0, 0)),
                  pl.BlockSpec(memory_space=pl.ANY)],
        out_specs=pl.BlockSpec(memory_space=pl.ANY),
        scratch_shapes=[pltpu.VMEM((2, tm, d), F32), pltpu.VMEM((2, tm, d), F32),
                        pltpu.SemaphoreType.DMA((2,)), pltpu.SemaphoreType.DMA((2,))],
    )
    return pl.pallas_call(
        functools.partial(_moe_kernel, tm=tm),
        grid_spec=grid_spec,
        out_shape=jax.ShapeDtypeStruct((out_rows, d), F32),
        compiler_params=_cparams(("arbitrary",)),
        name="moe_experts",
    )(tile_expert, n_used, src_tok, dst_row, row_w, wg, wu, wd, h)


def _expert_rows(h, ids, wts, wg, wu, wd, tm=256):
    n, d = h.shape
    n_assign = n * TOP_K
    n_tiles = n_assign // tm + N_EXPERTS
    eid = ids[:, :TOP_K].reshape(n_assign)
    gate = wts[:, :TOP_K].reshape(n_assign)
    a_idx = jnp.arange(n_assign, dtype=jnp.int32)
    eid_sorted, a_sorted, gate_sorted = lax.sort((eid, a_idx, gate), num_keys=1, is_stable=True)
    counts = jnp.sum(eid[:, None] == jnp.arange(N_EXPERTS)[None, :], axis=0).astype(jnp.int32)
    padded = ((counts + tm - 1) // tm) * tm
    pad_end = jnp.cumsum(padded)
    pad_start = pad_end - padded
    start = jnp.cumsum(counts) - counts
    n_used = (pad_end[-1] // tm).astype(jnp.int32)
    tile_start = jnp.minimum(jnp.arange(n_tiles, dtype=jnp.int32), n_used - 1) * tm
    tile_expert = jnp.searchsorted(pad_end, tile_start, side="right").astype(jnp.int32)
    row = jnp.arange(n_tiles * tm, dtype=jnp.int32)
    row_expert = tile_expert[row // tm]
    rank = row - pad_start[row_expert]
    valid = jnp.logical_and(rank < counts[row_expert], row < n_used * tm)
    src = jnp.clip(start[row_expert] + rank, 0, n_assign - 1)
    assign = a_sorted[src]
    src_tok = jnp.where(valid, assign // TOP_K, 0)
    dst_row = jnp.where(valid, (assign % TOP_K) * n + assign // TOP_K, n_assign + row % tm)
    row_w = jnp.where(valid, gate_sorted[src], 0.0)
    return _moe(tile_expert, n_used[None], src_tok, dst_row, h, row_w[:, None], wg, wu, wd, n_assign + tm, tm)


def _final_kernel(x_ref, y0_ref, y1_ref, g_ref, o_ref):
    o_ref[...] = _rms(x_ref[...] + (y0_ref[...] + y1_ref[...]), g_ref[...])


def _final(x1, y_rows, gain, tm=512):
    n, d = x1.shape
    spec = pl.BlockSpec((tm, d), lambda i: (i, 0))
    return pl.pallas_call(
        _final_kernel,
        grid=(n // tm,),
        in_specs=[spec, spec, pl.BlockSpec((tm, d), lambda i: (i + n // tm, 0)), pl.BlockSpec((1, d), lambda i: (0, 0))],
        out_specs=spec,
        out_shape=jax.ShapeDtypeStruct((n, d), F32),
        compiler_params=_cparams(("parallel",)),
        name="final_norm",
    )(x1, y_rows, y_rows, gain)


def _rotate_half_cols(w):
    half = QK_ROPE // 2
    return jnp.concatenate([-w[..., half:], w[..., :half]], axis=-1)


def _prep_w_in(w_in):
    d = w_in.shape[0]
    offs = np.cumsum([0, Q_LORA, KV_LORA, QK_ROPE, SB_HEADS * SB_HEAD_DIM, SB_KV_HEADS * SB_HEAD_DIM,
                      SB_KV_HEADS * SB_HEAD_DIM, d, d])
    c_q, c_kv, k_pe, sb_q, sb_k, sb_v, g_a, g_b = [w_in[:, offs[i]:offs[i + 1]] for i in range(8)]
    parts = [g_a, g_b, sb_q, c_q, c_kv, sb_k, sb_v, k_pe, _rotate_half_cols(k_pe)]
    used = sum(p.shape[1] for p in parts)
    parts.append(jnp.zeros((d, PROJ_WIDTH - used), w_in.dtype))
    return jnp.concatenate(parts, axis=1).astype(BF16)


def _prep_w_uq(w_uq):
    nope = w_uq[:, :, :QK_NOPE].reshape(Q_LORA, MLA_HEADS * QK_NOPE)
    pe = w_uq[:, :, QK_NOPE:]
    pe_rot = _rotate_half_cols(pe)
    return jnp.concatenate([nope, pe.reshape(Q_LORA, -1), pe_rot.reshape(Q_LORA, -1)], axis=1).astype(BF16)


def _rope_tables(positions):
    half = QK_ROPE // 2
    inv = 1.0 / (ROPE_THETA ** (jnp.arange(half, dtype=F32) / half))
    ang = positions.astype(F32)[:, None] * inv[None, :]
    cos, sin = jnp.cos(ang), jnp.sin(ang)
    cos2 = jnp.concatenate([cos, cos], axis=-1)
    sin2 = jnp.concatenate([sin, sin], axis=-1)
    return cos2, sin2


def _strict_upper(n):
    r = np.arange(n)
    return jnp.asarray((r[:, None] > r[None, :]).astype(np.float32), dtype=BF16)


def kernel(x_prompt, x_sample, cache_ckv, cache_kpe, cache_sb_k, cache_sb_v, page_table, norm_mix, w_in, q_norm,
           w_uq, kv_norm, w_uk, w_uv, w_pa, w_pb, w_o, norm_ffn, w_rg, b_rg, w_re, b_re, w_gate, w_up, w_down,
           norm_final):
    depth = w_in.shape[0]
    assert depth == 1
    batch, seq, d = x_prompt.shape
    dec_batch, dec_seq, _ = x_sample.shape
    n_prompt = batch * seq
    n_sample = dec_batch * dec_seq
    n = n_prompt + n_sample
    page = cache_ckv.shape[2]
    past_len = page_table.shape[1] * page
    post_tm = 256
    l = 0

    pos = jnp.concatenate([jnp.arange(seq), past_len + (jnp.arange(post_tm) % dec_seq)])
    cos2, sin2 = _rope_tables(pos)
    cosq = jnp.tile(cos2, (1, MLA_HEADS))
    sinq = jnp.tile(sin2, (1, MLA_HEADS))
    csk = jnp.concatenate([cos2, sin2], axis=1)
    w1 = _prep_w_in(w_in[l])
    wq = _prep_w_uq(w_uq[l])
    wuk_t = jnp.transpose(w_uk[l], (1, 2, 0)).astype(BF16)
    wuv = jnp.transpose(w_uv[l], (1, 0, 2)).astype(BF16)
    rw = 128
    w_router = jnp.concatenate([w_rg[l], w_re[l], jnp.zeros((d, rw - N_GROUPS - N_EXPERTS), F32)], axis=1)
    b_router = jnp.concatenate([b_rg[l], b_re[l], jnp.zeros((rw - N_GROUPS - N_EXPERTS,), F32)])[None, :]

    x = jnp.concatenate([x_prompt.reshape(n_prompt, d), x_sample.reshape(n_sample, d)], axis=0)
    proj = _inproj(x, norm_mix[l][None, :], w1)
    (ckv, kpe, sbk, sbv, ckvb, kpeb, sbkb, sbvb, qlat, qpe, sbq) = _post(
        proj, q_norm[l][None, :], kv_norm[l][None, :], wq, wuk_t, cosq, sinq, csk, n_prompt, seq, tm=post_tm)

    sb_t = 256
    oa_p = _mla_prompt(qlat, qpe, ckvb, kpeb, wuv, batch, seq)
    ob_p = _sb_prompt(sbq, sbkb, sbvb, _strict_upper(sb_t), batch, seq, t=sb_t)

    def per_seq(a):
        a = a[:, n_prompt:].reshape(a.shape[0], dec_batch, dec_seq, a.shape[2])
        return jnp.transpose(a, (1, 0, 2, 3)).reshape(dec_batch, a.shape[0] * dec_seq, a.shape[3])

    n_pool = cache_ckv.shape[1]
    kvw = SB_KV_HEADS * SB_HEAD_DIM
    o_lat_s = _mla_paged(page_table, per_seq(qlat), per_seq(qpe), ckv[n_prompt:], kpe[n_prompt:],
                         cache_ckv[l], jnp.swapaxes(cache_kpe[l], 1, 2))
    o_lat_s = jnp.transpose(o_lat_s.reshape(dec_batch, MLA_HEADS, dec_seq, KV_LORA), (1, 0, 2, 3))
    oa_s = _uv(o_lat_s.reshape(MLA_HEADS, n_sample, KV_LORA), wuv)
    ob_s = _sb_paged(page_table, per_seq(sbq), sbk[n_prompt:], sbv[n_prompt:],
                     cache_sb_k[l].reshape(n_pool, page * SB_KV_HEADS, SB_HEAD_DIM),
                     cache_sb_v[l].reshape(n_pool, page * SB_KV_HEADS, SB_HEAD_DIM),
                     _strict_upper(page))
    ob_s = jnp.transpose(ob_s.reshape(dec_batch, SB_HEADS, dec_seq, SB_HEAD_DIM), (0, 2, 1, 3))
    ob_s = ob_s.reshape(n_sample, SB_HEADS * SB_HEAD_DIM)

    o_a = jnp.concatenate([oa_p, oa_s], axis=0)
    o_b = jnp.concatenate([ob_p, ob_s], axis=0)
    merged = _merge(o_a, o_b, proj, w_pa[l].astype(BF16), w_pb[l].astype(BF16))
    x1, h2, logits = _outproj(merged, x, w_o[l].astype(BF16), norm_ffn[l][None, :], w_router, b_router)
    ids, wts = _route(logits)

    y_rows = _expert_rows(h2, ids, wts, w_gate[l], w_up[l], w_down[l])
    y = _final(x1, y_rows, norm_final[None, :])

    def rows_p(a, *tail):
        return a[:n_prompt].reshape((depth, batch, seq) + tail)

    def rows_s(a, *tail):
        return a[n_prompt:].reshape((depth, dec_batch, dec_seq) + tail)

    return (y[:n_prompt].reshape(batch, seq, d), y[n_prompt:].reshape(dec_batch, dec_seq, d),
            rows_p(ckv, KV_LORA), rows_p(kpe, QK_ROPE),
            rows_p(sbk, SB_KV_HEADS, SB_HEAD_DIM), rows_p(sbv, SB_KV_HEADS, SB_HEAD_DIM),
            rows_s(ckv, KV_LORA), rows_s(kpe, QK_ROPE),
            rows_s(sbk, SB_KV_HEADS, SB_HEAD_DIM), rows_s(sbv, SB_KV_HEADS, SB_HEAD_DIM))
```

```python
import functools

import jax
import jax.numpy as jnp
import numpy as np
from jax import lax
from jax.experimental import pallas as pl
from jax.experimental.pallas import tpu as pltpu

F32 = jnp.float32
BF16 = jnp.bfloat16

EPS = 1e-6
ROPE_THETA = 10000.0
MLA_HEADS = 8
Q_LORA = 512
KV_LORA = 512
QK_NOPE = 128
QK_ROPE = 64
V_HEAD = 128
MLA_SCALE = (QK_NOPE + QK_ROPE) ** -0.5
SB_HEADS = 8
SB_KV_HEADS = 2
SB_GROUP = SB_HEADS // SB_KV_HEADS
SB_HEAD_DIM = 128
SB_SCALE = SB_HEAD_DIM ** -0.5
N_GROUPS = 4
EXPERTS_PER_GROUP = 8
N_EXPERTS = N_GROUPS * EXPERTS_PER_GROUP
TOP_K = 2
NEG_BIG = -1e30

COL_GATE_A = 0
COL_GATE_B = 2048
COL_SBQ = 4096
COL_CQ = 5120
COL_CKV = 5632
COL_SBKV = 6144
COL_KPE = 6656
PROJ_WIDTH = 7168

VMEM_LIMIT = 48 * 1024 * 1024
PAGE_RING = 3


def _cparams(semantics):
    return pltpu.CompilerParams(dimension_semantics=semantics, vmem_limit_bytes=VMEM_LIMIT)


def _dot(a, b):
    return jnp.dot(a, b, preferred_element_type=F32)


def _dot_nt(a, b):
    return lax.dot_general(a, b, (((1,), (1,)), ((), ())), preferred_element_type=F32)


def _rms(x, gain):
    return x * lax.rsqrt(jnp.mean(x * x, axis=-1, keepdims=True) + EPS) * gain


def _sigmoid(x):
    return 1.0 / (1.0 + jnp.exp(-x))


def _inproj_kernel(x_ref, g_ref, w_ref, o_ref, h_scr):
    @pl.when(pl.program_id(1) == 0)
    def _():
        h_scr[...] = _rms(x_ref[...], g_ref[...]).astype(BF16)

    o_ref[...] = _dot(h_scr[...], w_ref[...])


def _inproj(x, gain, w1, tm=512, tn=1024):
    n, d = x.shape
    width = w1.shape[1]
    return pl.pallas_call(
        _inproj_kernel,
        grid=(n // tm, width // tn),
        in_specs=[pl.BlockSpec((tm, d), lambda i, j: (i, 0)),
                  pl.BlockSpec((1, d), lambda i, j: (0, 0)),
                  pl.BlockSpec((d, tn), lambda i, j: (0, j))],
        out_specs=pl.BlockSpec((tm, tn), lambda i, j: (i, j)),
        out_shape=jax.ShapeDtypeStruct((n, width), F32),
        scratch_shapes=[pltpu.VMEM((tm, d), BF16)],
        compiler_params=_cparams(("parallel", "arbitrary")),
        name="inproj",
    )(x, gain, w1)


def _post_kernel(cq_ref, ckv_ref, sbkv_ref, kpe_ref, sbq_ref, qn_ref, kvn_ref, wq_ref, wuk_ref,
                 cosq_ref, sinq_ref, csk_ref,
                 ckv_o, kpe_o, sbk_o, sbv_o, ckvb_o, kpeb_o, sbkb_o, sbvb_o, qlat_o, qpe_o, sbqb_o):
    cqn = _rms(cq_ref[...], qn_ref[...]).astype(BF16)
    q = _dot(cqn, wq_ref[...])
    for h in range(MLA_HEADS):
        q_nope = q[:, QK_NOPE * h:QK_NOPE * (h + 1)].astype(BF16)
        qlat_o[h] = _dot(q_nope, wuk_ref[h]).astype(BF16)
    pe0 = MLA_HEADS * QK_NOPE
    pe1 = pe0 + MLA_HEADS * QK_ROPE
    roped = q[:, pe0:pe1] * cosq_ref[...] + q[:, pe1:pe1 + MLA_HEADS * QK_ROPE] * sinq_ref[...]
    for h in range(MLA_HEADS):
        qpe_o[h] = roped[:, QK_ROPE * h:QK_ROPE * (h + 1)].astype(BF16)

    ckv = _rms(ckv_ref[...], kvn_ref[...])
    ckv_o[...] = ckv
    ckvb_o[...] = ckv.astype(BF16)

    prod = kpe_ref[...] * csk_ref[...]
    kpe = prod[:, :QK_ROPE] + prod[:, QK_ROPE:]
    kpe_o[...] = kpe
    kpeb_o[...] = kpe.astype(BF16)

    kvw = SB_KV_HEADS * SB_HEAD_DIM
    sbk = sbkv_ref[:, :kvw]
    sbv = sbkv_ref[:, kvw:]
    sbk_o[...] = sbk
    sbv_o[...] = sbv
    sbkb_o[...] = sbk.astype(BF16)
    sbvb_o[...] = sbv.astype(BF16)
    for h in range(SB_HEADS):
        sbqb_o[h] = sbq_ref[:, SB_HEAD_DIM * h:SB_HEAD_DIM * (h + 1)].astype(BF16)


def _post(proj, q_norm, kv_norm, wq, wuk_t, cosq, sinq, csk, n_prompt, seq, tm=256):
    n = proj.shape[0]
    prompt_blocks = n_prompt // tm
    table_blocks = seq // tm

    def tab(i):
        return (jnp.where(i < prompt_blocks, i % table_blocks, table_blocks), 0)

    def col(width, offset):
        return pl.BlockSpec((tm, width), lambda i: (i, offset // width))

    def full(shape):
        return pl.BlockSpec(shape, lambda i: (0,) * len(shape))

    def rows(width, dtype):
        return pl.BlockSpec((tm, width), lambda i: (i, 0)), jax.ShapeDtypeStruct((n, width), dtype)

    def heads(width):
        return (pl.BlockSpec((MLA_HEADS, tm, width), lambda i: (0, i, 0)),
                jax.ShapeDtypeStruct((MLA_HEADS, n, width), BF16))

    kvw = SB_KV_HEADS * SB_HEAD_DIM
    outs = [rows(KV_LORA, F32), rows(QK_ROPE, F32), rows(kvw, F32), rows(kvw, F32),
            rows(KV_LORA, BF16), rows(QK_ROPE, BF16), rows(kvw, BF16), rows(kvw, BF16),
            heads(KV_LORA), heads(QK_ROPE), heads(SB_HEAD_DIM)]
    return pl.pallas_call(
        _post_kernel,
        grid=(n // tm,),
        in_specs=[col(Q_LORA, COL_CQ), col(KV_LORA, COL_CKV), col(2 * kvw, COL_SBKV),
                  col(2 * QK_ROPE, COL_KPE), col(SB_HEADS * SB_HEAD_DIM, COL_SBQ),
                  full((1, Q_LORA)), full((1, KV_LORA)), full(wq.shape), full(wuk_t.shape),
                  pl.BlockSpec((tm, MLA_HEADS * QK_ROPE), tab),
                  pl.BlockSpec((tm, MLA_HEADS * QK_ROPE), tab),
                  pl.BlockSpec((tm, 2 * QK_ROPE), tab)],
        out_specs=[o[0] for o in outs],
        out_shape=[o[1] for o in outs],
        compiler_params=_cparams(("parallel",)),
        name="post_proj",
    )(proj, proj, proj, proj, proj, q_norm, kv_norm, wq, wuk_t, cosq, sinq, csk)


def _mla_prompt_kernel(qlat_ref, qpe_ref, ckv_ref, kpe_ref, wuv_ref, o_ref, m_scr, l_scr, acc_scr, *, tq, tk):
    qi = pl.program_id(1)
    ki = pl.program_id(2)
    rows = MLA_HEADS * tq
    last = ((qi + 1) * tq - 1) // tk

    @pl.when(ki == 0)
    def _():
        m_scr[...] = jnp.full(m_scr.shape, NEG_BIG, F32)
        l_scr[...] = jnp.zeros(l_scr.shape, F32)
        acc_scr[...] = jnp.zeros(acc_scr.shape, F32)

    @pl.when(ki <= last)
    def _():
        q = qlat_ref[...].reshape(rows, KV_LORA)
        qp = qpe_ref[...].reshape(rows, QK_ROPE)
        k = ckv_ref[...]
        s = (_dot_nt(q, k) + _dot_nt(qp, kpe_ref[...])) * MLA_SCALE
        q_pos = qi * tq + (lax.broadcasted_iota(jnp.int32, (rows, tk), 0) & (tq - 1))
        k_pos = ki * tk + lax.broadcasted_iota(jnp.int32, (rows, tk), 1)
        s = jnp.where(k_pos <= q_pos, s, -jnp.inf)
        m_prev = m_scr[...]
        m_new = jnp.maximum(m_prev, jnp.max(s, axis=-1, keepdims=True))
        alpha = jnp.exp(m_prev - m_new)
        p = jnp.exp(s - m_new)
        l_scr[...] = alpha * l_scr[...] + jnp.sum(p, axis=-1, keepdims=True)
        acc_scr[...] = alpha * acc_scr[...] + _dot(p.astype(BF16), k)
        m_scr[...] = m_new

    @pl.when(ki == pl.num_programs(2) - 1)
    def _():
        o_lat = (acc_scr[...] / l_scr[...]).astype(BF16)
        for h in range(MLA_HEADS):
            o_ref[:, V_HEAD * h:V_HEAD * (h + 1)] = _dot(o_lat[h * tq:(h + 1) * tq], wuv_ref[h]).astype(BF16)


def _mla_prompt(qlat, qpe, ckvb, kpeb, wuv, batch, seq, tq=128, tk=512):
    nq = seq // tq
    nk = seq // tk
    rows = MLA_HEADS * tq

    def kmap(b, qi, ki):
        return (b * nk + jnp.minimum(ki, ((qi + 1) * tq - 1) // tk), 0)

    return pl.pallas_call(
        functools.partial(_mla_prompt_kernel, tq=tq, tk=tk),
        grid=(batch, nq, nk),
        in_specs=[pl.BlockSpec((MLA_HEADS, tq, KV_LORA), lambda b, qi, ki: (0, b * nq + qi, 0)),
                  pl.BlockSpec((MLA_HEADS, tq, QK_ROPE), lambda b, qi, ki: (0, b * nq + qi, 0)),
                  pl.BlockSpec((tk, KV_LORA), kmap),
                  pl.BlockSpec((tk, QK_ROPE), kmap),
                  pl.BlockSpec(wuv.shape, lambda b, qi, ki: (0, 0, 0))],
        out_specs=pl.BlockSpec((tq, MLA_HEADS * V_HEAD), lambda b, qi, ki: (b * nq + qi, 0)),
        out_shape=jax.ShapeDtypeStruct((batch * seq, MLA_HEADS * V_HEAD), BF16),
        scratch_shapes=[pltpu.VMEM((rows, 1), F32), pltpu.VMEM((rows, 1), F32), pltpu.VMEM((rows, KV_LORA), F32)],
        compiler_params=_cparams(("parallel", "parallel", "arbitrary")),
        name="mla_prompt",
    )(qlat, qpe, ckvb, kpeb, wuv)


def _sb_block(z, mask, upper, carry):
    soft = jnp.log(1.0 + jnp.exp(-jnp.abs(z)))
    log_keep = -(jnp.maximum(z, 0.0) + soft)
    log_beta = log_keep + z
    if mask is not None:
        log_keep = jnp.where(mask, log_keep, 0.0)
    hi = log_keep.astype(BF16)
    lo = (log_keep - hi.astype(F32)).astype(BF16)
    newer = _dot(hi, upper) + _dot(lo, upper)
    w = jnp.exp(log_beta + newer + carry)
    if mask is not None:
        w = jnp.where(mask, w, 0.0)
    return w, jnp.sum(log_keep, axis=-1, keepdims=True)


def _sb_prompt_kernel(q_ref, k_ref, v_ref, up_ref, o_ref, carry_scr, acc_scr, *, t):
    qi = pl.program_id(2)
    step = pl.program_id(3)
    rows = SB_GROUP * t

    @pl.when(step == 0)
    def _():
        carry_scr[...] = jnp.zeros(carry_scr.shape, F32)
        acc_scr[...] = jnp.zeros(acc_scr.shape, F32)

    def block(masked):
        q = q_ref[...].reshape(rows, SB_HEAD_DIM)
        z = _dot_nt(q, k_ref[...]) * SB_SCALE
        mask = None
        if masked:
            q_pos = lax.broadcasted_iota(jnp.int32, (rows, t), 0) & (t - 1)
            k_pos = lax.broadcasted_iota(jnp.int32, (rows, t), 1)
            mask = k_pos < q_pos
        w, total = _sb_block(z, mask, up_ref[...], carry_scr[...])
        acc_scr[...] += _dot(w.astype(BF16), v_ref[...])
        carry_scr[...] += total

    @pl.when(step == 0)
    def _():
        block(True)

    @pl.when(jnp.logical_and(step > 0, step <= qi))
    def _():
        block(False)

    @pl.when(step == pl.num_programs(3) - 1)
    def _():
        for g in range(SB_GROUP):
            o_ref[:, SB_HEAD_DIM * g:SB_HEAD_DIM * (g + 1)] = acc_scr[g * t:(g + 1) * t].astype(BF16)


def _sb_prompt(sbq, sbkb, sbvb, upper, batch, seq, t=256):
    nq = seq // t
    rows = SB_GROUP * t

    def kvmap(b, kv, qi, step):
        return (b * nq + jnp.maximum(qi - step, 0), kv)

    return pl.pallas_call(
        functools.partial(_sb_prompt_kernel, t=t),
        grid=(batch, SB_KV_HEADS, nq, nq),
        in_specs=[pl.BlockSpec((SB_GROUP, t, SB_HEAD_DIM), lambda b, kv, qi, step: (kv, b * nq + qi, 0)),
                  pl.BlockSpec((t, SB_HEAD_DIM), kvmap),
                  pl.BlockSpec((t, SB_HEAD_DIM), kvmap),
                  pl.BlockSpec((t, t), lambda b, kv, qi, step: (0, 0))],
        out_specs=pl.BlockSpec((t, SB_GROUP * SB_HEAD_DIM), lambda b, kv, qi, step: (b * nq + qi, kv)),
        out_shape=jax.ShapeDtypeStruct((batch * seq, SB_HEADS * SB_HEAD_DIM), BF16),
        scratch_shapes=[pltpu.VMEM((rows, 1), F32), pltpu.VMEM((rows, SB_HEAD_DIM), F32)],
        compiler_params=_cparams(("parallel", "parallel", "parallel", "arbitrary")),
        name="sb_prompt",
    )(sbq, sbkb, sbvb, upper)


def _pad_rows(x, rows):
    return jnp.concatenate([x, jnp.zeros((rows - x.shape[0], x.shape[1]), x.dtype)], axis=0)


def _mla_paged_kernel(pt_ref, q_ref, qpe_ref, w2_ref, ckvn_ref, kpen_ref, ckv_hbm, kpe_hbm, o_ref,
                      ckv_buf, kpe_buf, sem, m_scr, l_scr, acc_scr, *, pages, page, t_new):
    step = pl.program_id(1)
    steps = pl.num_programs(1)
    g = pl.program_id(0) * steps + step
    last = pl.num_programs(0) * steps - 1
    slot = lax.rem(g, PAGE_RING)
    rows = MLA_HEADS * t_new
    qp = qpe_ref[0]

    def page_copies(g_idx, slot_idx):
        out = []
        for i in range(pages):
            pid = 0 if g_idx is None else pt_ref[g_idx * pages + i]
            out.append(pltpu.make_async_copy(ckv_hbm.at[pid], ckv_buf.at[slot_idx, i], sem.at[slot_idx, 0]))
            out.append(pltpu.make_async_copy(kpe_hbm.at[pid], kpe_buf.at[slot_idx, i], sem.at[slot_idx, 1]))
        return out

    @pl.when(g == 0)
    def _():
        for ahead in range(PAGE_RING - 1):
            for i, c in enumerate(page_copies(ahead, ahead)):
                c.start(priority=(i // 2) % 2)

    for c in page_copies(None, slot):
        c.wait()
    ahead = g + (PAGE_RING - 1)
    for i, c in enumerate(page_copies(jnp.minimum(ahead, last), lax.rem(ahead, PAGE_RING))):
        c.start(priority=(i // 2) % 2)

    def update(s, kcat):
        width = s.shape[1] // len(kcat)
        m_prev = m_scr[...]
        m_new = jnp.maximum(m_prev, jnp.max(s, axis=-1, keepdims=True))
        alpha = jnp.exp(m_prev - m_new)
        p = jnp.exp(s - m_new)
        l_scr[...] = alpha * l_scr[...] + jnp.sum(p, axis=-1, keepdims=True)
        p = p.astype(BF16)
        pv = _dot(p[:, :width], kcat[0])
        for i in range(1, len(kcat)):
            pv += _dot(p[:, i * width:(i + 1) * width], kcat[i])
        acc_scr[...] = alpha * acc_scr[...] + pv
        m_scr[...] = m_new

    @pl.when(step == 0)
    def _():
        m_scr[...] = jnp.full(m_scr.shape, NEG_BIG, F32)
        l_scr[...] = jnp.zeros(l_scr.shape, F32)
        acc_scr[...] = jnp.zeros(acc_scr.shape, F32)
        k = _pad_rows(ckvn_ref[...], page).astype(BF16)
        kp = _pad_rows(kpen_ref[...], page).astype(BF16)
        s = (_dot_nt(q_ref[0], k) + _dot_nt(qp, kp)) * MLA_SCALE
        t_q = lax.broadcasted_iota(jnp.int32, (rows, page), 0) & (t_new - 1)
        t_k = lax.broadcasted_iota(jnp.int32, (rows, page), 1)
        update(jnp.where(t_k <= t_q, s, -jnp.inf), [k])

    ks = [ckv_buf[slot, i].astype(BF16) for i in range(pages)]
    kps = [kpe_buf[slot, i].astype(BF16) for i in range(pages)]
    pairs = pages // 2
    lhs = jnp.concatenate([jnp.concatenate([ks[2 * j], ks[2 * j + 1]], axis=1) for j in range(pairs)], axis=0)
    st = _dot(lhs, w2_ref[0])
    nope = []
    for j in range(pairs):
        both = st[j * page:(j + 1) * page].T
        nope += [both[:rows], both[rows:]]
    s = jnp.concatenate([nope[i] + _dot(qp, kps[i]) for i in range(pages)], axis=1) * MLA_SCALE
    update(s, [jnp.concatenate([ks[2 * j], ks[2 * j + 1]], axis=0) for j in range(pairs)])

    @pl.when(step == steps - 1)
    def _():
        o_ref[0] = (acc_scr[...] / l_scr[...]).astype(BF16)

    @pl.when(g == last)
    def _():
        for extra in range(1, PAGE_RING):
            for c in page_copies(None, lax.rem(g + extra, PAGE_RING)):
                c.wait()


def _mla_paged(page_table, q, qpe, ckv_new, kpe_new, cache_ckv, cache_kpe_t, pages=16):
    batch, rows, _ = q.shape
    t_new = rows // MLA_HEADS
    n_pages = page_table.shape[1]
    page = cache_ckv.shape[1]
    assert 2 * rows == page and n_pages % pages == 0 and pages % 2 == 0 and batch * (n_pages // pages) >= PAGE_RING
    steps = n_pages // pages
    pt = page_table.reshape(-1)
    q_t = jnp.swapaxes(q, 1, 2)
    zero = jnp.zeros_like(q_t)
    w2 = jnp.concatenate([jnp.concatenate([q_t, zero], axis=2), jnp.concatenate([zero, q_t], axis=2)], axis=1)

    grid_spec = pltpu.PrefetchScalarGridSpec(
        num_scalar_prefetch=1,
        grid=(batch, steps),
        in_specs=[pl.BlockSpec((1, rows, KV_LORA), lambda b, s, pt_ref: (b, 0, 0)),
                  pl.BlockSpec((1, rows, QK_ROPE), lambda b, s, pt_ref: (b, 0, 0)),
                  pl.BlockSpec((1, 2 * KV_LORA, 2 * rows), lambda b, s, pt_ref: (b, 0, 0)),
                  pl.BlockSpec((t_new, KV_LORA), lambda b, s, pt_ref: (b, 0)),
                  pl.BlockSpec((t_new, QK_ROPE), lambda b, s, pt_ref: (b, 0)),
                  pl.BlockSpec(memory_space=pl.ANY),
                  pl.BlockSpec(memory_space=pl.ANY)],
        out_specs=pl.BlockSpec((1, rows, KV_LORA), lambda b, s, pt_ref: (b, 0, 0)),
        scratch_shapes=[pltpu.VMEM((PAGE_RING, pages, page, KV_LORA), F32),
                        pltpu.VMEM((PAGE_RING, pages, QK_ROPE, page), F32),
                        pltpu.SemaphoreType.DMA((PAGE_RING, 2)),
                        pltpu.VMEM((rows, 1), F32), pltpu.VMEM((rows, 1), F32), pltpu.VMEM((rows, KV_LORA), F32)],
    )
    return pl.pallas_call(
        functools.partial(_mla_paged_kernel, pages=pages, page=page, t_new=t_new),
        grid_spec=grid_spec,
        out_shape=jax.ShapeDtypeStruct((batch, rows, KV_LORA), BF16),
        compiler_params=_cparams(("arbitrary", "arbitrary")),
        name="mla_paged",
    )(pt, q, qpe, w2, ckv_new, kpe_new, cache_ckv, cache_kpe_t)


def _uv_kernel(o_ref, w_ref, out_ref):
    for h in range(MLA_HEADS):
        out_ref[:, V_HEAD * h:V_HEAD * (h + 1)] = _dot(o_ref[h], w_ref[h]).astype(BF16)


def _uv(o_lat, wuv):
    n = o_lat.shape[1]
    return pl.pallas_call(
        _uv_kernel,
        grid=(1,),
        in_specs=[pl.BlockSpec(o_lat.shape, lambda i: (0, 0, 0)), pl.BlockSpec(wuv.shape, lambda i: (0, 0, 0))],
        out_specs=pl.BlockSpec((n, MLA_HEADS * V_HEAD), lambda i: (0, 0)),
        out_shape=jax.ShapeDtypeStruct((n, MLA_HEADS * V_HEAD), BF16),
        compiler_params=_cparams(("arbitrary",)),
        name="mla_value_up",
    )(o_lat, wuv)


def _sb_paged_kernel(pt_ref, q_ref, kn_ref, vn_ref, up_ref, k_hbm, v_hbm, o_ref,
                     k_buf, v_buf, sem, carry_scr, acc_scr, *, pages, page, t_new):
    step = pl.program_id(1)
    steps = pl.num_programs(1)
    g = pl.program_id(0) * steps + step
    last = pl.num_programs(0) * steps - 1
    slot = lax.rem(g, PAGE_RING)
    half = SB_GROUP * t_new
    rows = SB_KV_HEADS * half
    q = q_ref[0]

    def page_copies(g_idx, slot_idx):
        base = 0
        if g_idx is not None:
            base = (lax.div(g_idx, steps) * steps + (steps - 1 - lax.rem(g_idx, steps))) * pages
        out = []
        for i in range(pages):
            pid = 0 if g_idx is None else pt_ref[base + i]
            out.append(pltpu.make_async_copy(k_hbm.at[pid], k_buf.at[slot_idx, i], sem.at[slot_idx, 0]))
            out.append(pltpu.make_async_copy(v_hbm.at[pid], v_buf.at[slot_idx, i], sem.at[slot_idx, 1]))
        return out

    @pl.when(g == 0)
    def _():
        for ahead in range(PAGE_RING - 1):
            for i, c in enumerate(page_copies(ahead, ahead)):
                c.start(priority=(i // 2) % 2)

    for c in page_copies(None, slot):
        c.wait()
    ahead = g + (PAGE_RING - 1)
    for i, c in enumerate(page_copies(jnp.minimum(ahead, last), lax.rem(ahead, PAGE_RING))):
        c.start(priority=(i // 2) % 2)

    def attend(ks, vs, mask):
        n = len(ks)
        z = jnp.concatenate([_dot_nt(q[kv * half:(kv + 1) * half], k[:, kv * SB_HEAD_DIM:(kv + 1) * SB_HEAD_DIM])
                             for k in ks for kv in range(SB_KV_HEADS)], axis=0) * SB_SCALE
        soft = jnp.log(1.0 + jnp.exp(-jnp.abs(z)))
        log_keep = -(jnp.maximum(z, 0.0) + soft)
        log_beta = log_keep + z
        if mask is not None:
            log_keep = jnp.where(mask, log_keep, 0.0)
        hi = log_keep.astype(BF16)
        lo = (log_keep - hi.astype(F32)).astype(BF16)
        newer = _dot(jnp.concatenate([hi, lo], axis=0), up_ref[...])
        newer = newer[:n * rows] + newer[n * rows:]
        total = jnp.sum(log_keep, axis=-1, keepdims=True)
        carry = carry_scr[...]
        carries = []
        for b in range(n):
            carries.append(carry)
            carry = carry + total[b * rows:(b + 1) * rows]
        carry_scr[...] = carry
        w = jnp.exp(log_beta + newer + jnp.concatenate(carries, axis=0))
        if mask is not None:
            w = jnp.where(mask, w, 0.0)
        w = w.astype(BF16)
        pv = _dot(w[:rows], vs[0])
        for b in range(1, n):
            pv += _dot(w[b * rows:(b + 1) * rows], vs[b])
        acc_scr[...] += pv

    @pl.when(step == 0)
    def _():
        carry_scr[...] = jnp.zeros(carry_scr.shape, F32)
        acc_scr[...] = jnp.zeros(acc_scr.shape, F32)
        t_q = lax.broadcasted_iota(jnp.int32, (rows, page), 0) & (t_new - 1)
        t_k = lax.broadcasted_iota(jnp.int32, (rows, page), 1)
        attend([_pad_rows(kn_ref[...], page).astype(BF16)], [_pad_rows(vn_ref[...], page).astype(BF16)], t_k < t_q)

    def heads(buf, i):
        return jnp.concatenate([buf[slot, i, pl.ds(kv, page, stride=SB_KV_HEADS), :] for kv in range(SB_KV_HEADS)],
                               axis=1).astype(BF16)

    order = range(pages - 1, -1, -1)
    attend([heads(k_buf, i) for i in order], [heads(v_buf, i) for i in order], None)

    @pl.when(step == steps - 1)
    def _():
        acc = acc_scr[...]
        o_ref[0] = jnp.concatenate([acc[kv * half:(kv + 1) * half, kv * SB_HEAD_DIM:(kv + 1) * SB_HEAD_DIM]
                                    for kv in range(SB_KV_HEADS)], axis=0).astype(BF16)

    @pl.when(g == last)
    def _():
        for extra in range(1, PAGE_RING):
            for c in page_copies(None, lax.rem(g + extra, PAGE_RING)):
                c.wait()


def _sb_paged(page_table, q, k_new, v_new, cache_k, cache_v, upper, pages=16):
    batch, rows, _ = q.shape
    t_new = rows // SB_HEADS
    n_pages = page_table.shape[1]
    page = cache_k.shape[1] // SB_KV_HEADS
    steps = n_pages // pages
    kvw = SB_KV_HEADS * SB_HEAD_DIM
    pt = page_table.reshape(-1)

    assert n_pages % pages == 0
    page_rows = page * SB_KV_HEADS
    grid_spec = pltpu.PrefetchScalarGridSpec(
        num_scalar_prefetch=1,
        grid=(batch, steps),
        in_specs=[pl.BlockSpec((1, rows, SB_HEAD_DIM), lambda b, s, pt_ref: (b, 0, 0)),
                  pl.BlockSpec((t_new, kvw), lambda b, s, pt_ref: (b, 0)),
                  pl.BlockSpec((t_new, kvw), lambda b, s, pt_ref: (b, 0)),
                  pl.BlockSpec((page, page), lambda b, s, pt_ref: (0, 0)),
                  pl.BlockSpec(memory_space=pl.ANY),
                  pl.BlockSpec(memory_space=pl.ANY)],
        out_specs=pl.BlockSpec((1, rows, SB_HEAD_DIM), lambda b, s, pt_ref: (b, 0, 0)),
        scratch_shapes=[pltpu.VMEM((PAGE_RING, pages, page_rows, SB_HEAD_DIM), F32),
                        pltpu.VMEM((PAGE_RING, pages, page_rows, SB_HEAD_DIM), F32),
                        pltpu.SemaphoreType.DMA((PAGE_RING, 2)),
                        pltpu.VMEM((rows, 1), F32), pltpu.VMEM((rows, kvw), F32)],
    )
    return pl.pallas_call(
        functools.partial(_sb_paged_kernel, pages=pages, page=page, t_new=t_new),
        grid_spec=grid_spec,
        out_shape=jax.ShapeDtypeStruct((batch, rows, SB_HEAD_DIM), BF16),
        compiler_params=_cparams(("arbitrary", "arbitrary")),
        name="sb_paged",
    )(pt, q, k_new, v_new, upper, cache_k, cache_v)


def _merge_kernel(oa_ref, ob_ref, ga_ref, gb_ref, wpa_ref, wpb_ref, o_ref):
    a = _dot(oa_ref[...], wpa_ref[...])
    b = _dot(ob_ref[...], wpb_ref[...])
    o_ref[...] = (_sigmoid(ga_ref[...]) * a + _sigmoid(gb_ref[...]) * b).astype(BF16)


def _merge(o_a, o_b, proj, wpa, wpb, tm=512, tn=1024):
    n, inner = o_a.shape
    d = wpa.shape[1]
    return pl.pallas_call(
        _merge_kernel,
        grid=(n // tm, d // tn),
        in_specs=[pl.BlockSpec((tm, inner), lambda i, j: (i, 0)),
                  pl.BlockSpec((tm, inner), lambda i, j: (i, 0)),
                  pl.BlockSpec((tm, tn), lambda i, j: (i, COL_GATE_A // tn + j)),
                  pl.BlockSpec((tm, tn), lambda i, j: (i, COL_GATE_B // tn + j)),
                  pl.BlockSpec((inner, tn), lambda i, j: (0, j)),
                  pl.BlockSpec((inner, tn), lambda i, j: (0, j))],
        out_specs=pl.BlockSpec((tm, tn), lambda i, j: (i, j)),
        out_shape=jax.ShapeDtypeStruct((n, d), BF16),
        compiler_params=_cparams(("parallel", "arbitrary")),
        name="gated_merge",
    )(o_a, o_b, proj, proj, wpa, wpb)


def _outproj_kernel(m_ref, x_ref, wo_ref, g_ref, wr_ref, br_ref, x1_ref, h_ref, lg_ref):
    x1 = x_ref[...] + _dot(m_ref[...], wo_ref[...])
    x1_ref[...] = x1
    h = _rms(x1, g_ref[...])
    h_ref[...] = h
    h_hi = h.astype(BF16)
    h_lo = (h - h_hi.astype(F32)).astype(BF16)
    wr = wr_ref[...]
    w_hi = wr.astype(BF16)
    w_lo = (wr - w_hi.astype(F32)).astype(BF16)
    lg_ref[...] = _dot(h_hi, w_hi) + _dot(h_lo, w_hi) + _dot(h_hi, w_lo) + br_ref[...]


def _outproj(merged, x, wo, g_ffn, w_router, b_router, tm=256):
    n, d = x.shape
    rw = w_router.shape[1]
    return pl.pallas_call(
        _outproj_kernel,
        grid=(n // tm,),
        in_specs=[pl.BlockSpec((tm, d), lambda i: (i, 0)),
                  pl.BlockSpec((tm, d), lambda i: (i, 0)),
                  pl.BlockSpec((d, d), lambda i: (0, 0)),
                  pl.BlockSpec((1, d), lambda i: (0, 0)),
                  pl.BlockSpec((d, rw), lambda i: (0, 0)),
                  pl.BlockSpec((1, rw), lambda i: (0, 0))],
        out_specs=[pl.BlockSpec((tm, d), lambda i: (i, 0)),
                   pl.BlockSpec((tm, d), lambda i: (i, 0)),
                   pl.BlockSpec((tm, rw), lambda i: (i, 0))],
        out_shape=[jax.ShapeDtypeStruct((n, d), F32), jax.ShapeDtypeStruct((n, d), F32),
                   jax.ShapeDtypeStruct((n, rw), F32)],
        compiler_params=_cparams(("parallel",)),
        name="out_proj_router",
    )(merged, x, wo, g_ffn, w_router, b_router)


def _route_kernel(lg_ref, id_ref, w_ref):
    lg = lg_ref[...]
    col = lax.broadcasted_iota(jnp.int32, lg.shape, 1)
    big = jnp.int32(1 << 20)

    def first_argmax(v):
        m = jnp.max(v, axis=-1, keepdims=True)
        return m, jnp.min(jnp.where(v == m, col, big), axis=-1, keepdims=True)

    gl = jnp.where(col < N_GROUPS, lg, -jnp.inf)
    g_max, g_idx = first_argmax(gl)
    g_w = 1.0 / jnp.sum(jnp.exp(gl - g_max), axis=-1, keepdims=True)
    lo = N_GROUPS + g_idx * EXPERTS_PER_GROUP
    el = jnp.where(jnp.logical_and(col >= lo, col < lo + EXPERTS_PER_GROUP), lg, -jnp.inf)
    v1, i1 = first_argmax(el)
    v2, i2 = first_argmax(jnp.where(col == i1, -jnp.inf, el))
    e2 = jnp.exp(v2 - v1)
    w1 = 1.0 / (1.0 + e2) * g_w
    w2 = e2 / (1.0 + e2) * g_w
    id_ref[...] = jnp.where(col == 0, i1 - N_GROUPS, jnp.where(col == 1, i2 - N_GROUPS, 0))
    w_ref[...] = jnp.where(col == 0, w1, jnp.where(col == 1, w2, 0.0))


def _route(logits, tm=512):
    n, rw = logits.shape
    spec = pl.BlockSpec((tm, rw), lambda i: (i, 0))
    return pl.pallas_call(
        _route_kernel,
        grid=(n // tm,),
        in_specs=[spec],
        out_specs=[spec, spec],
        out_shape=[jax.ShapeDtypeStruct((n, rw), jnp.int32), jax.ShapeDtypeStruct((n, rw), F32)],
        compiler_params=_cparams(("parallel",)),
        name="route",
    )(logits)


def _moe_kernel(te_ref, nu_ref, src_ref, dst_ref, rw_ref, wg_ref, wu_ref, wd_ref, h_hbm, y_hbm,
                x_buf, o_buf, sem_in, sem_out, *, tm):
    tile = pl.program_id(0)
    last = pl.num_programs(0) - 1
    slot = lax.rem(tile, 2)

    def row_reads(tile_idx, slot_idx):
        out = []
        for r in range(tm):
            src = 0 if tile_idx is None else src_ref[tile_idx * tm + r]
            out.append(pltpu.make_async_copy(h_hbm.at[pl.ds(src, 1)], x_buf.at[slot_idx, pl.ds(r, 1)],
                                             sem_in.at[slot_idx]))
        return out

    def row_writes(tile_idx, slot_idx):
        out = []
        for r in range(tm):
            dst = 0 if tile_idx is None else dst_ref[tile_idx * tm + r]
            out.append(pltpu.make_async_copy(o_buf.at[slot_idx, pl.ds(r, 1)], y_hbm.at[pl.ds(dst, 1)],
                                             sem_out.at[slot_idx]))
        return out

    @pl.when(tile == 0)
    def _():
        for r, c in enumerate(row_reads(0, 0)):
            c.start(priority=r % 2)

    for c in row_reads(None, slot):
        c.wait()
    for r, c in enumerate(row_reads(jnp.minimum(tile + 1, last), 1 - slot)):
        c.start(priority=r % 2)

    @pl.when(tile < nu_ref[0])
    def _():
        x = x_buf[slot].astype(BF16)
        g = _dot(x, wg_ref[0].astype(BF16))
        u = _dot(x, wu_ref[0].astype(BF16))
        hid = (g * _sigmoid(g)) * u * rw_ref[...]
        o_buf[slot] = _dot(hid.astype(BF16), wd_ref[0].astype(BF16))

    @pl.when(tile >= nu_ref[0])
    def _():
        o_buf[slot] = jnp.zeros(o_buf.shape[1:], F32)

    @pl.when(tile > 0)
    def _():
        for c in row_writes(None, 1 - slot):
            c.wait()

    for r, c in enumerate(row_writes(tile, slot)):
        c.start(priority=r % 2)

    @pl.when(tile == last)
    def _():
        for c in row_writes(None, slot):
            c.wait()
        for c in row_reads(None, 1 - slot):
            c.wait()


def _moe(tile_expert, n_used, src_tok, dst_row, h, row_w, wg, wu, wd, out_rows, tm):
    rows = src_tok.shape[0]
    d = h.shape[1]
    ff = wg.shape[2]
    grid_spec = pltpu.PrefetchScalarGridSpec(
        num_scalar_prefetch=4,
        grid=(rows // tm,),
        in_specs=[pl.BlockSpec((tm, 1), lambda t, te, nu, src, dst: (t, 0)),
                  pl.BlockSpec((1, d, ff), lambda t, te, nu, src, dst: (te[t], 0, 0)),
                  pl.BlockSpec((1, d, ff), lambda t, te, nu, src, dst: (te[t], 0, 0)),
                  pl.BlockSpec((1, ff, d), lambda t, te, nu, src, dst: (te[t], 0, 0)),
                  pl.BlockSpec(memory_space=pl.ANY)],
        out_specs=pl.BlockSpec(memory_space=pl.ANY),
        scratch_shapes=[pltpu.VMEM((2, tm, d), F32), pltpu.VMEM((2, tm, d), F32),
                        pltpu.SemaphoreType.DMA((2,)), pltpu.SemaphoreType.DMA((2,))],
    )
    return pl.pallas_call(
        functools.partial(_moe_kernel, tm=tm),
        grid_spec=grid_spec,
        out_shape=jax.ShapeDtypeStruct((out_rows, d), F32),
        compiler_params=_cparams(("arbitrary",)),
        name="moe_experts",
    )(tile_expert, n_used, src_tok, dst_row, row_w, wg, wu, wd, h)


def _expert_rows(h, ids, wts, wg, wu, wd, tm=256):
    n, d = h.shape
    n_assign = n * TOP_K
    n_tiles = n_assign // tm + N_EXPERTS
    eid = ids[:, :TOP_K].reshape(n_assign)
    gate = wts[:, :TOP_K].reshape(n_assign)
    a_idx = jnp.arange(n_assign, dtype=jnp.int32)
    eid_sorted, a_sorted, gate_sorted = lax.sort((eid, a_idx, gate), num_keys=1, is_stable=True)
    bounds = jnp.searchsorted(eid_sorted, jnp.arange(N_EXPERTS + 1, dtype=jnp.int32), side="left").astype(jnp.int32)
    counts = bounds[1:] - bounds[:-1]
    padded = ((counts + tm - 1) // tm) * tm
    pad_end = jnp.cumsum(padded)
    pad_start = pad_end - padded
    start = jnp.cumsum(counts) - counts
    n_used = (pad_end[-1] // tm).astype(jnp.int32)
    tile_start = jnp.minimum(jnp.arange(n_tiles, dtype=jnp.int32), n_used - 1) * tm
    tile_expert = jnp.searchsorted(pad_end, tile_start, side="right").astype(jnp.int32)
    row = jnp.arange(n_tiles * tm, dtype=jnp.int32)
    row_expert = tile_expert[row // tm]
    rank = row - pad_start[row_expert]
    valid = jnp.logical_and(rank < counts[row_expert], row < n_used * tm)
    src = jnp.clip(start[row_expert] + rank, 0, n_assign - 1)
    assign = a_sorted[src]
    src_tok = jnp.where(valid, assign // TOP_K, 0)
    dst_row = jnp.where(valid, (assign % TOP_K) * n + assign // TOP_K, n_assign + row % tm)
    row_w = jnp.where(valid, gate_sorted[src], 0.0)
    return _moe(tile_expert, n_used[None], src_tok, dst_row, h, row_w[:, None], wg, wu, wd, n_assign + tm, tm)


def _final_kernel(x_ref, y0_ref, y1_ref, g_ref, o_ref):
    o_ref[...] = _rms(x_ref[...] + (y0_ref[...] + y1_ref[...]), g_ref[...])


def _final(x1, y_rows, gain, tm=512):
    n, d = x1.shape
    spec = pl.BlockSpec((tm, d), lambda i: (i, 0))
    return pl.pallas_call(
        _final_kernel,
        grid=(n // tm,),
        in_specs=[spec, spec, pl.BlockSpec((tm, d), lambda i: (i + n // tm, 0)), pl.BlockSpec((1, d), lambda i: (0, 0))],
        out_specs=spec,
        out_shape=jax.ShapeDtypeStruct((n, d), F32),
        compiler_params=_cparams(("parallel",)),
        name="final_norm",
    )(x1, y_rows, y_rows, gain)


def _rotate_half_cols(w):
    half = QK_ROPE // 2
    return jnp.concatenate([-w[..., half:], w[..., :half]], axis=-1)


def _prep_w_in(w_in):
    d = w_in.shape[0]
    offs = np.cumsum([0, Q_LORA, KV_LORA, QK_ROPE, SB_HEADS * SB_HEAD_DIM, SB_KV_HEADS * SB_HEAD_DIM,
                      SB_KV_HEADS * SB_HEAD_DIM, d, d])
    c_q, c_kv, k_pe, sb_q, sb_k, sb_v, g_a, g_b = [w_in[:, offs[i]:offs[i + 1]] for i in range(8)]
    parts = [g_a, g_b, sb_q, c_q, c_kv, sb_k, sb_v, k_pe, _rotate_half_cols(k_pe)]
    used = sum(p.shape[1] for p in parts)
    parts.append(jnp.zeros((d, PROJ_WIDTH - used), w_in.dtype))
    return jnp.concatenate(parts, axis=1).astype(BF16)


def _prep_w_uq(w_uq):
    nope = w_uq[:, :, :QK_NOPE].reshape(Q_LORA, MLA_HEADS * QK_NOPE)
    pe = w_uq[:, :, QK_NOPE:]
    pe_rot = _rotate_half_cols(pe)
    return jnp.concatenate([nope, pe.reshape(Q_LORA, -1), pe_rot.reshape(Q_LORA, -1)], axis=1).astype(BF16)


def _rope_tables(positions):
    half = QK_ROPE // 2
    inv = 1.0 / (ROPE_THETA ** (jnp.arange(half, dtype=F32) / half))
    ang = positions.astype(F32)[:, None] * inv[None, :]
    cos, sin = jnp.cos(ang), jnp.sin(ang)
    cos2 = jnp.concatenate([cos, cos], axis=-1)
    sin2 = jnp.concatenate([sin, sin], axis=-1)
    return cos2, sin2


def _strict_upper(n):
    r = np.arange(n)
    return jnp.asarray((r[:, None] > r[None, :]).astype(np.float32), dtype=BF16)


def kernel(x_prompt, x_sample, cache_ckv, cache_kpe, cache_sb_k, cache_sb_v, page_table, norm_mix, w_in, q_norm,
           w_uq, kv_norm, w_uk, w_uv, w_pa, w_pb, w_o, norm_ffn, w_rg, b_rg, w_re, b_re, w_gate, w_up, w_down,
           norm_final):
    depth = w_in.shape[0]
    assert depth == 1
    batch, seq, d = x_prompt.shape
    dec_batch, dec_seq, _ = x_sample.shape
    n_prompt = batch * seq
    n_sample = dec_batch * dec_seq
    n = n_prompt + n_sample
    page = cache_ckv.shape[2]
    past_len = page_table.shape[1] * page
    post_tm = 256
    l = 0

    pos = jnp.concatenate([jnp.arange(seq), past_len + (jnp.arange(post_tm) % dec_seq)])
    cos2, sin2 = _rope_tables(pos)
    cosq = jnp.tile(cos2, (1, MLA_HEADS))
    sinq = jnp.tile(sin2, (1, MLA_HEADS))
    csk = jnp.concatenate([cos2, sin2], axis=1)
    w1 = _prep_w_in(w_in[l])
    wq = _prep_w_uq(w_uq[l])
    wuk_t = jnp.transpose(w_uk[l], (1, 2, 0)).astype(BF16)
    wuv = jnp.transpose(w_uv[l], (1, 0, 2)).astype(BF16)
    rw = 128
    w_router = jnp.concatenate([w_rg[l], w_re[l], jnp.zeros((d, rw - N_GROUPS - N_EXPERTS), F32)], axis=1)
    b_router = jnp.concatenate([b_rg[l], b_re[l], jnp.zeros((rw - N_GROUPS - N_EXPERTS,), F32)])[None, :]

    x = jnp.concatenate([x_prompt.reshape(n_prompt, d), x_sample.reshape(n_sample, d)], axis=0)
    proj = _inproj(x, norm_mix[l][None, :], w1)
    (ckv, kpe, sbk, sbv, ckvb, kpeb, sbkb, sbvb, qlat, qpe, sbq) = _post(
        proj, q_norm[l][None, :], kv_norm[l][None, :], wq, wuk_t, cosq, sinq, csk, n_prompt, seq, tm=post_tm)

    sb_t = 256
    oa_p = _mla_prompt(qlat, qpe, ckvb, kpeb, wuv, batch, seq)
    ob_p = _sb_prompt(sbq, sbkb, sbvb, _strict_upper(sb_t), batch, seq, t=sb_t)

    def per_seq(a):
        a = a[:, n_prompt:].reshape(a.shape[0], dec_batch, dec_seq, a.shape[2])
        return jnp.transpose(a, (1, 0, 2, 3)).reshape(dec_batch, a.shape[0] * dec_seq, a.shape[3])

    n_pool = cache_ckv.shape[1]
    kvw = SB_KV_HEADS * SB_HEAD_DIM
    o_lat_s = _mla_paged(page_table, per_seq(qlat), per_seq(qpe), ckv[n_prompt:], kpe[n_prompt:],
                         cache_ckv[l], jnp.swapaxes(cache_kpe[l], 1, 2))
    o_lat_s = jnp.transpose(o_lat_s.reshape(dec_batch, MLA_HEADS, dec_seq, KV_LORA), (1, 0, 2, 3))
    oa_s = _uv(o_lat_s.reshape(MLA_HEADS, n_sample, KV_LORA), wuv)
    ob_s = _sb_paged(page_table, per_seq(sbq), sbk[n_prompt:], sbv[n_prompt:],
                     cache_sb_k[l].reshape(n_pool, page * SB_KV_HEADS, SB_HEAD_DIM),
                     cache_sb_v[l].reshape(n_pool, page * SB_KV_HEADS, SB_HEAD_DIM),
                     _strict_upper(page))
    ob_s = jnp.transpose(ob_s.reshape(dec_batch, SB_HEADS, dec_seq, SB_HEAD_DIM), (0, 2, 1, 3))
    ob_s = ob_s.reshape(n_sample, SB_HEADS * SB_HEAD_DIM)

    o_a = jnp.concatenate([oa_p, oa_s], axis=0)
    o_b = jnp.concatenate([ob_p, ob_s], axis=0)
    merged = _merge(o_a, o_b, proj, w_pa[l].astype(BF16), w_pb[l].astype(BF16))
    x1, h2, logits = _outproj(merged, x, w_o[l].astype(BF16), norm_ffn[l][None, :], w_router, b_router)
    ids, wts = _route(logits)

    y_rows = _expert_rows(h2, ids, wts, w_gate[l], w_up[l], w_down[l])
    y = _final(x1, y_rows, norm_final[None, :])

    def rows_p(a, *tail):
        return a[:n_prompt].reshape((depth, batch, seq) + tail)

    def rows_s(a, *tail):
        return a[n_prompt:].reshape((depth, dec_batch, dec_seq) + tail)

    return (y[:n_prompt].reshape(batch, seq, d), y[n_prompt:].reshape(dec_batch, dec_seq, d),
            rows_p(ckv, KV_LORA), rows_p(kpe, QK_ROPE),
            rows_p(sbk, SB_KV_HEADS, SB_HEAD_DIM), rows_p(sbv, SB_KV_HEADS, SB_HEAD_DIM),
            rows_s(ckv, KV_LORA), rows_s(kpe, QK_ROPE),
            rows_s(sbk, SB_KV_HEADS, SB_HEAD_DIM), rows_s(sbv, SB_KV_HEADS, SB_HEAD_DIM))
```

```python
import functools

import jax
import jax.numpy as jnp
import numpy as np
from jax import lax
from jax.experimental import pallas as pl
from jax.experimental.pallas import tpu as pltpu

F32 = jnp.float32
BF16 = jnp.bfloat16

EPS = 1e-6
ROPE_THETA = 10000.0
MLA_HEADS = 8
Q_LORA = 512
KV_LORA = 512
QK_NOPE = 128
QK_ROPE = 64
V_HEAD = 128
MLA_SCALE = (QK_NOPE + QK_ROPE) ** -0.5
SB_HEADS = 8
SB_KV_HEADS = 2
SB_GROUP = SB_HEADS // SB_KV_HEADS
SB_HEAD_DIM = 128
SB_SCALE = SB_HEAD_DIM ** -0.5
N_GROUPS = 4
EXPERTS_PER_GROUP = 8
N_EXPERTS = N_GROUPS * EXPERTS_PER_GROUP
TOP_K = 2
NEG_BIG = -1e30

COL_GATE_A = 0
COL_GATE_B = 2048
COL_SBQ = 4096
COL_CQ = 5120
COL_CKV = 5632
COL_SBKV = 6144
COL_KPE = 6656
PROJ_WIDTH = 7168

VMEM_LIMIT = 48 * 1024 * 1024
PAGE_RING = 3


def _cparams(semantics):
    return pltpu.CompilerParams(dimension_semantics=semantics, vmem_limit_bytes=VMEM_LIMIT)


def _dot(a, b):
    return jnp.dot(a, b, preferred_element_type=F32)


def _dot_nt(a, b):
    return lax.dot_general(a, b, (((1,), (1,)), ((), ())), preferred_element_type=F32)


def _rms(x, gain):
    return x * lax.rsqrt(jnp.mean(x * x, axis=-1, keepdims=True) + EPS) * gain


def _sigmoid(x):
    return 1.0 / (1.0 + jnp.exp(-x))


def _inproj_kernel(x_ref, g_ref, w_ref, o_ref, h_scr):
    @pl.when(pl.program_id(1) == 0)
    def _():
        h_scr[...] = _rms(x_ref[...], g_ref[...]).astype(BF16)

    o_ref[...] = _dot(h_scr[...], w_ref[...])


def _inproj(x, gain, w1, tm=512, tn=1024):
    n, d = x.shape
    width = w1.shape[1]
    return pl.pallas_call(
        _inproj_kernel,
        grid=(n // tm, width // tn),
        in_specs=[pl.BlockSpec((tm, d), lambda i, j: (i, 0)),
                  pl.BlockSpec((1, d), lambda i, j: (0, 0)),
                  pl.BlockSpec((d, tn), lambda i, j: (0, j))],
        out_specs=pl.BlockSpec((tm, tn), lambda i, j: (i, j)),
        out_shape=jax.ShapeDtypeStruct((n, width), F32),
        scratch_shapes=[pltpu.VMEM((tm, d), BF16)],
        compiler_params=_cparams(("parallel", "arbitrary")),
        name="inproj",
    )(x, gain, w1)


def _post_kernel(cq_ref, ckv_ref, sbkv_ref, kpe_ref, sbq_ref, qn_ref, kvn_ref, wq_ref, wuk_ref,
                 cosq_ref, sinq_ref, csk_ref,
                 ckv_o, kpe_o, sbk_o, sbv_o, ckvb_o, kpeb_o, sbkb_o, sbvb_o, qlat_o, qpe_o, sbqb_o):
    cqn = _rms(cq_ref[...], qn_ref[...]).astype(BF16)
    q = _dot(cqn, wq_ref[...])
    for h in range(MLA_HEADS):
        q_nope = q[:, QK_NOPE * h:QK_NOPE * (h + 1)].astype(BF16)
        qlat_o[h] = _dot(q_nope, wuk_ref[h]).astype(BF16)
    pe0 = MLA_HEADS * QK_NOPE
    pe1 = pe0 + MLA_HEADS * QK_ROPE
    roped = q[:, pe0:pe1] * cosq_ref[...] + q[:, pe1:pe1 + MLA_HEADS * QK_ROPE] * sinq_ref[...]
    for h in range(MLA_HEADS):
        qpe_o[h] = roped[:, QK_ROPE * h:QK_ROPE * (h + 1)].astype(BF16)

    ckv = _rms(ckv_ref[...], kvn_ref[...])
    ckv_o[...] = ckv
    ckvb_o[...] = ckv.astype(BF16)

    prod = kpe_ref[...] * csk_ref[...]
    kpe = prod[:, :QK_ROPE] + prod[:, QK_ROPE:]
    kpe_o[...] = kpe
    kpeb_o[...] = kpe.astype(BF16)

    kvw = SB_KV_HEADS * SB_HEAD_DIM
    sbk = sbkv_ref[:, :kvw]
    sbv = sbkv_ref[:, kvw:]
    sbk_o[...] = sbk
    sbv_o[...] = sbv
    sbkb_o[...] = sbk.astype(BF16)
    sbvb_o[...] = sbv.astype(BF16)
    for h in range(SB_HEADS):
        sbqb_o[h] = sbq_ref[:, SB_HEAD_DIM * h:SB_HEAD_DIM * (h + 1)].astype(BF16)


def _post(proj, q_norm, kv_norm, wq, wuk_t, cosq, sinq, csk, n_prompt, seq, tm=256):
    n = proj.shape[0]
    prompt_blocks = n_prompt // tm
    table_blocks = seq // tm

    def tab(i):
        return (jnp.where(i < prompt_blocks, i % table_blocks, table_blocks), 0)

    def col(width, offset):
        return pl.BlockSpec((tm, width), lambda i: (i, offset // width))

    def full(shape):
        return pl.BlockSpec(shape, lambda i: (0,) * len(shape))

    def rows(width, dtype):
        return pl.BlockSpec((tm, width), lambda i: (i, 0)), jax.ShapeDtypeStruct((n, width), dtype)

    def heads(width):
        return (pl.BlockSpec((MLA_HEADS, tm, width), lambda i: (0, i, 0)),
                jax.ShapeDtypeStruct((MLA_HEADS, n, width), BF16))

    kvw = SB_KV_HEADS * SB_HEAD_DIM
    outs = [rows(KV_LORA, F32), rows(QK_ROPE, F32), rows(kvw, F32), rows(kvw, F32),
            rows(KV_LORA, BF16), rows(QK_ROPE, BF16), rows(kvw, BF16), rows(kvw, BF16),
            heads(KV_LORA), heads(QK_ROPE), heads(SB_HEAD_DIM)]
    return pl.pallas_call(
        _post_kernel,
        grid=(n // tm,),
        in_specs=[col(Q_LORA, COL_CQ), col(KV_LORA, COL_CKV), col(2 * kvw, COL_SBKV),
                  col(2 * QK_ROPE, COL_KPE), col(SB_HEADS * SB_HEAD_DIM, COL_SBQ),
                  full((1, Q_LORA)), full((1, KV_LORA)), full(wq.shape), full(wuk_t.shape),
                  pl.BlockSpec((tm, MLA_HEADS * QK_ROPE), tab),
                  pl.BlockSpec((tm, MLA_HEADS * QK_ROPE), tab),
                  pl.BlockSpec((tm, 2 * QK_ROPE), tab)],
        out_specs=[o[0] for o in outs],
        out_shape=[o[1] for o in outs],
        compiler_params=_cparams(("parallel",)),
        name="post_proj",
    )(proj, proj, proj, proj, proj, q_norm, kv_norm, wq, wuk_t, cosq, sinq, csk)


def _mla_prompt_kernel(qlat_ref, qpe_ref, ckv_ref, kpe_ref, wuv_ref, o_ref, m_scr, l_scr, acc_scr, *, tq, tk):
    qi = pl.program_id(1)
    ki = pl.program_id(2)
    rows = MLA_HEADS * tq
    last = ((qi + 1) * tq - 1) // tk

    @pl.when(ki == 0)
    def _():
        m_scr[...] = jnp.full(m_scr.shape, NEG_BIG, F32)
        l_scr[...] = jnp.zeros(l_scr.shape, F32)
        acc_scr[...] = jnp.zeros(acc_scr.shape, F32)

    @pl.when(ki <= last)
    def _():
        q = qlat_ref[...].reshape(rows, KV_LORA)
        qp = qpe_ref[...].reshape(rows, QK_ROPE)
        k = ckv_ref[...]
        s = (_dot_nt(q, k) + _dot_nt(qp, kpe_ref[...])) * MLA_SCALE
        q_pos = qi * tq + (lax.broadcasted_iota(jnp.int32, (rows, tk), 0) & (tq - 1))
        k_pos = ki * tk + lax.broadcasted_iota(jnp.int32, (rows, tk), 1)
        s = jnp.where(k_pos <= q_pos, s, -jnp.inf)
        m_prev = m_scr[...]
        m_new = jnp.maximum(m_prev, jnp.max(s, axis=-1, keepdims=True))
        alpha = jnp.exp(m_prev - m_new)
        p = jnp.exp(s - m_new)
        l_scr[...] = alpha * l_scr[...] + jnp.sum(p, axis=-1, keepdims=True)
        acc_scr[...] = alpha * acc_scr[...] + _dot(p.astype(BF16), k)
        m_scr[...] = m_new

    @pl.when(ki == pl.num_programs(2) - 1)
    def _():
        o_lat = (acc_scr[...] / l_scr[...]).astype(BF16)
        for h in range(MLA_HEADS):
            o_ref[:, V_HEAD * h:V_HEAD * (h + 1)] = _dot(o_lat[h * tq:(h + 1) * tq], wuv_ref[h]).astype(BF16)


def _mla_prompt(qlat, qpe, ckvb, kpeb, wuv, batch, seq, tq=128, tk=512):
    nq = seq // tq
    nk = seq // tk
    rows = MLA_HEADS * tq

    def kmap(b, qi, ki):
        return (b * nk + jnp.minimum(ki, ((qi + 1) * tq - 1) // tk), 0)

    return pl.pallas_call(
        functools.partial(_mla_prompt_kernel, tq=tq, tk=tk),
        grid=(batch, nq, nk),
        in_specs=[pl.BlockSpec((MLA_HEADS, tq, KV_LORA), lambda b, qi, ki: (0, b * nq + qi, 0)),
                  pl.BlockSpec((MLA_HEADS, tq, QK_ROPE), lambda b, qi, ki: (0, b * nq + qi, 0)),
                  pl.BlockSpec((tk, KV_LORA), kmap),
                  pl.BlockSpec((tk, QK_ROPE), kmap),
                  pl.BlockSpec(wuv.shape, lambda b, qi, ki: (0, 0, 0))],
        out_specs=pl.BlockSpec((tq, MLA_HEADS * V_HEAD), lambda b, qi, ki: (b * nq + qi, 0)),
        out_shape=jax.ShapeDtypeStruct((batch * seq, MLA_HEADS * V_HEAD), BF16),
        scratch_shapes=[pltpu.VMEM((rows, 1), F32), pltpu.VMEM((rows, 1), F32), pltpu.VMEM((rows, KV_LORA), F32)],
        compiler_params=_cparams(("parallel", "parallel", "arbitrary")),
        name="mla_prompt",
    )(qlat, qpe, ckvb, kpeb, wuv)


def _sb_block(z, mask, upper, carry):
    soft = jnp.log(1.0 + jnp.exp(-jnp.abs(z)))
    log_keep = -(jnp.maximum(z, 0.0) + soft)
    log_beta = log_keep + z
    if mask is not None:
        log_keep = jnp.where(mask, log_keep, 0.0)
    hi = log_keep.astype(BF16)
    lo = (log_keep - hi.astype(F32)).astype(BF16)
    newer = _dot(hi, upper) + _dot(lo, upper)
    w = jnp.exp(log_beta + newer + carry)
    if mask is not None:
        w = jnp.where(mask, w, 0.0)
    return w, jnp.sum(log_keep, axis=-1, keepdims=True)


def _sb_prompt_kernel(q_ref, k_ref, v_ref, up_ref, o_ref, carry_scr, acc_scr, *, t):
    qi = pl.program_id(2)
    step = pl.program_id(3)
    rows = SB_GROUP * t

    @pl.when(step == 0)
    def _():
        carry_scr[...] = jnp.zeros(carry_scr.shape, F32)
        acc_scr[...] = jnp.zeros(acc_scr.shape, F32)

    def block(masked):
        q = q_ref[...].reshape(rows, SB_HEAD_DIM)
        z = _dot_nt(q, k_ref[...]) * SB_SCALE
        mask = None
        if masked:
            q_pos = lax.broadcasted_iota(jnp.int32, (rows, t), 0) & (t - 1)
            k_pos = lax.broadcasted_iota(jnp.int32, (rows, t), 1)
            mask = k_pos < q_pos
        w, total = _sb_block(z, mask, up_ref[...], carry_scr[...])
        acc_scr[...] += _dot(w.astype(BF16), v_ref[...])
        carry_scr[...] += total

    @pl.when(step == 0)
    def _():
        block(True)

    @pl.when(jnp.logical_and(step > 0, step <= qi))
    def _():
        block(False)

    @pl.when(step == pl.num_programs(3) - 1)
    def _():
        for g in range(SB_GROUP):
            o_ref[:, SB_HEAD_DIM * g:SB_HEAD_DIM * (g + 1)] = acc_scr[g * t:(g + 1) * t].astype(BF16)


def _sb_prompt(sbq, sbkb, sbvb, upper, batch, seq, t=256):
    nq = seq // t
    rows = SB_GROUP * t

    def kvmap(b, kv, qi, step):
        return (b * nq + jnp.maximum(qi - step, 0), kv)

    return pl.pallas_call(
        functools.partial(_sb_prompt_kernel, t=t),
        grid=(batch, SB_KV_HEADS, nq, nq),
        in_specs=[pl.BlockSpec((SB_GROUP, t, SB_HEAD_DIM), lambda b, kv, qi, step: (kv, b * nq + qi, 0)),
                  pl.BlockSpec((t, SB_HEAD_DIM), kvmap),
                  pl.BlockSpec((t, SB_HEAD_DIM), kvmap),
                  pl.BlockSpec((t, t), lambda b, kv, qi, step: (0, 0))],
        out_specs=pl.BlockSpec((t, SB_GROUP * SB_HEAD_DIM), lambda b, kv, qi, step: (b * nq + qi, kv)),
        out_shape=jax.ShapeDtypeStruct((batch * seq, SB_HEADS * SB_HEAD_DIM), BF16),
        scratch_shapes=[pltpu.VMEM((rows, 1), F32), pltpu.VMEM((rows, SB_HEAD_DIM), F32)],
        compiler_params=_cparams(("parallel", "parallel", "parallel", "arbitrary")),
        name="sb_prompt",
    )(sbq, sbkb, sbvb, upper)


def _pad_rows(x, rows):
    return jnp.concatenate([x, jnp.zeros((rows - x.shape[0], x.shape[1]), x.dtype)], axis=0)


def _mla_paged_kernel(pt_ref, q_ref, qpe_ref, w2_ref, ckvn_ref, kpen_ref, ckv_hbm, kpe_hbm, o_ref,
                      ckv_buf, kpe_buf, sem, m_scr, l_scr, acc_scr, *, pages, page, t_new):
    step = pl.program_id(1)
    steps = pl.num_programs(1)
    g = pl.program_id(0) * steps + step
    last = pl.num_programs(0) * steps - 1
    slot = lax.rem(g, PAGE_RING)
    rows = MLA_HEADS * t_new
    qp = qpe_ref[0]

    def page_copies(g_idx, slot_idx):
        out = []
        for i in range(pages):
            pid = 0 if g_idx is None else pt_ref[g_idx * pages + i]
            out.append(pltpu.make_async_copy(ckv_hbm.at[pid], ckv_buf.at[slot_idx, i], sem.at[slot_idx, 0]))
            out.append(pltpu.make_async_copy(kpe_hbm.at[pid], kpe_buf.at[slot_idx, i], sem.at[slot_idx, 1]))
        return out

    @pl.when(g == 0)
    def _():
        for ahead in range(PAGE_RING - 1):
            for i, c in enumerate(page_copies(ahead, ahead)):
                c.start(priority=(i // 2) % 2)

    for c in page_copies(None, slot):
        c.wait()
    ahead = g + (PAGE_RING - 1)
    for i, c in enumerate(page_copies(jnp.minimum(ahead, last), lax.rem(ahead, PAGE_RING))):
        c.start(priority=(i // 2) % 2)

    def update(s, kcat):
        width = s.shape[1] // len(kcat)
        m_prev = m_scr[...]
        m_new = jnp.maximum(m_prev, jnp.max(s, axis=-1, keepdims=True))
        alpha = jnp.exp(m_prev - m_new)
        p = jnp.exp(s - m_new)
        l_scr[...] = alpha * l_scr[...] + jnp.sum(p, axis=-1, keepdims=True)
        p = p.astype(BF16)
        pv = _dot(p[:, :width], kcat[0])
        for i in range(1, len(kcat)):
            pv += _dot(p[:, i * width:(i + 1) * width], kcat[i])
        acc_scr[...] = alpha * acc_scr[...] + pv
        m_scr[...] = m_new

    @pl.when(step == 0)
    def _():
        m_scr[...] = jnp.full(m_scr.shape, NEG_BIG, F32)
        l_scr[...] = jnp.zeros(l_scr.shape, F32)
        acc_scr[...] = jnp.zeros(acc_scr.shape, F32)
        k = _pad_rows(ckvn_ref[...], page).astype(BF16)
        kp = _pad_rows(kpen_ref[...], page).astype(BF16)
        s = (_dot_nt(q_ref[0], k) + _dot_nt(qp, kp)) * MLA_SCALE
        t_q = lax.broadcasted_iota(jnp.int32, (rows, page), 0) & (t_new - 1)
        t_k = lax.broadcasted_iota(jnp.int32, (rows, page), 1)
        update(jnp.where(t_k <= t_q, s, -jnp.inf), [k])

    ks = [ckv_buf[slot, i].astype(BF16) for i in range(pages)]
    kps = [kpe_buf[slot, i].astype(BF16) for i in range(pages)]
    pairs = pages // 2
    lhs = jnp.concatenate([jnp.concatenate([ks[2 * j], ks[2 * j + 1]], axis=1) for j in range(pairs)], axis=0)
    st = _dot(lhs, w2_ref[0])
    nope = []
    for j in range(pairs):
        both = st[j * page:(j + 1) * page].T
        nope += [both[:rows], both[rows:]]
    s = jnp.concatenate([nope[i] + _dot(qp, kps[i]) for i in range(pages)], axis=1) * MLA_SCALE
    update(s, [jnp.concatenate([ks[2 * j], ks[2 * j + 1]], axis=0) for j in range(pairs)])

    @pl.when(step == steps - 1)
    def _():
        o_ref[0] = (acc_scr[...] / l_scr[...]).astype(BF16)

    @pl.when(g == last)
    def _():
        for extra in range(1, PAGE_RING):
            for c in page_copies(None, lax.rem(g + extra, PAGE_RING)):
                c.wait()


def _mla_paged(page_table, q, qpe, ckv_new, kpe_new, cache_ckv, cache_kpe_t, pages=16):
    batch, rows, _ = q.shape
    t_new = rows // MLA_HEADS
    n_pages = page_table.shape[1]
    page = cache_ckv.shape[1]
    assert 2 * rows == page and n_pages % pages == 0 and pages % 2 == 0 and batch * (n_pages // pages) >= PAGE_RING
    steps = n_pages // pages
    pt = page_table.reshape(-1)
    q_t = jnp.swapaxes(q, 1, 2)
    zero = jnp.zeros_like(q_t)
    w2 = jnp.concatenate([jnp.concatenate([q_t, zero], axis=2), jnp.concatenate([zero, q_t], axis=2)], axis=1)

    grid_spec = pltpu.PrefetchScalarGridSpec(
        num_scalar_prefetch=1,
        grid=(batch, steps),
        in_specs=[pl.BlockSpec((1, rows, KV_LORA), lambda b, s, pt_ref: (b, 0, 0)),
                  pl.BlockSpec((1, rows, QK_ROPE), lambda b, s, pt_ref: (b, 0, 0)),
                  pl.BlockSpec((1, 2 * KV_LORA, 2 * rows), lambda b, s, pt_ref: (b, 0, 0)),
                  pl.BlockSpec((t_new, KV_LORA), lambda b, s, pt_ref: (b, 0)),
                  pl.BlockSpec((t_new, QK_ROPE), lambda b, s, pt_ref: (b, 0)),
                  pl.BlockSpec(memory_space=pl.ANY),
                  pl.BlockSpec(memory_space=pl.ANY)],
        out_specs=pl.BlockSpec((1, rows, KV_LORA), lambda b, s, pt_ref: (b, 0, 0)),
        scratch_shapes=[pltpu.VMEM((PAGE_RING, pages, page, KV_LORA), F32),
                        pltpu.VMEM((PAGE_RING, pages, QK_ROPE, page), F32),
                        pltpu.SemaphoreType.DMA((PAGE_RING, 2)),
                        pltpu.VMEM((rows, 1), F32), pltpu.VMEM((rows, 1), F32), pltpu.VMEM((rows, KV_LORA), F32)],
    )
    return pl.pallas_call(
        functools.partial(_mla_paged_kernel, pages=pages, page=page, t_new=t_new),
        grid_spec=grid_spec,
        out_shape=jax.ShapeDtypeStruct((batch, rows, KV_LORA), BF16),
        compiler_params=_cparams(("arbitrary", "arbitrary")),
        name="mla_paged",
    )(pt, q, qpe, w2, ckv_new, kpe_new, cache_ckv, cache_kpe_t)


def _uv_kernel(o_ref, w_ref, out_ref):
    for h in range(MLA_HEADS):
        out_ref[:, V_HEAD * h:V_HEAD * (h + 1)] = _dot(o_ref[h], w_ref[h]).astype(BF16)


def _uv(o_lat, wuv):
    n = o_lat.shape[1]
    return pl.pallas_call(
        _uv_kernel,
        grid=(1,),
        in_specs=[pl.BlockSpec(o_lat.shape, lambda i: (0, 0, 0)), pl.BlockSpec(wuv.shape, lambda i: (0, 0, 0))],
        out_specs=pl.BlockSpec((n, MLA_HEADS * V_HEAD), lambda i: (0, 0)),
        out_shape=jax.ShapeDtypeStruct((n, MLA_HEADS * V_HEAD), BF16),
        compiler_params=_cparams(("arbitrary",)),
        name="mla_value_up",
    )(o_lat, wuv)


def _sb_paged_kernel(pt_ref, q_ref, kn_ref, vn_ref, up_ref, k_hbm, v_hbm, o_ref,
                     k_buf, v_buf, sem, carry_scr, acc_scr, *, pages, page, t_new):
    step = pl.program_id(1)
    steps = pl.num_programs(1)
    g = pl.program_id(0) * steps + step
    last = pl.num_programs(0) * steps - 1
    slot = lax.rem(g, PAGE_RING)
    half = SB_GROUP * t_new
    rows = SB_KV_HEADS * half
    q = q_ref[0]

    def page_copies(g_idx, slot_idx):
        base = 0
        if g_idx is not None:
            base = (lax.div(g_idx, steps) * steps + (steps - 1 - lax.rem(g_idx, steps))) * pages
        out = []
        for i in range(pages):
            pid = 0 if g_idx is None else pt_ref[base + i]
            out.append(pltpu.make_async_copy(k_hbm.at[pid], k_buf.at[slot_idx, i], sem.at[slot_idx, 0]))
            out.append(pltpu.make_async_copy(v_hbm.at[pid], v_buf.at[slot_idx, i], sem.at[slot_idx, 1]))
        return out

    @pl.when(g == 0)
    def _():
        for ahead in range(PAGE_RING - 1):
            for i, c in enumerate(page_copies(ahead, ahead)):
                c.start(priority=(i // 2) % 2)

    for c in page_copies(None, slot):
        c.wait()
    ahead = g + (PAGE_RING - 1)
    for i, c in enumerate(page_copies(jnp.minimum(ahead, last), lax.rem(ahead, PAGE_RING))):
        c.start(priority=(i // 2) % 2)

    def attend(ks, vs, mask):
        n = len(ks)
        z = jnp.concatenate([_dot_nt(q[kv * half:(kv + 1) * half], k[:, kv * SB_HEAD_DIM:(kv + 1) * SB_HEAD_DIM])
                             for k in ks for kv in range(SB_KV_HEADS)], axis=0) * SB_SCALE
        soft = jnp.log(1.0 + jnp.exp(-jnp.abs(z)))
        log_keep = -(jnp.maximum(z, 0.0) + soft)
        log_beta = log_keep + z
        if mask is not None:
            log_keep = jnp.where(mask, log_keep, 0.0)
        hi = log_keep.astype(BF16)
        lo = (log_keep - hi.astype(F32)).astype(BF16)
        newer = _dot(jnp.concatenate([hi, lo], axis=0), up_ref[...])
        newer = newer[:n * rows] + newer[n * rows:]
        total = jnp.sum(log_keep, axis=-1, keepdims=True)
        carry = carry_scr[...]
        carries = []
        for b in range(n):
            carries.append(carry)
            carry = carry + total[b * rows:(b + 1) * rows]
        carry_scr[...] = carry
        w = jnp.exp(log_beta + newer + jnp.concatenate(carries, axis=0))
        if mask is not None:
            w = jnp.where(mask, w, 0.0)
        w = w.astype(BF16)
        pv = _dot(w[:rows], vs[0])
        for b in range(1, n):
            pv += _dot(w[b * rows:(b + 1) * rows], vs[b])
        acc_scr[...] += pv

    @pl.when(step == 0)
    def _():
        carry_scr[...] = jnp.zeros(carry_scr.shape, F32)
        acc_scr[...] = jnp.zeros(acc_scr.shape, F32)
        t_q = lax.broadcasted_iota(jnp.int32, (rows, page), 0) & (t_new - 1)
        t_k = lax.broadcasted_iota(jnp.int32, (rows, page), 1)
        attend([_pad_rows(kn_ref[...], page).astype(BF16)], [_pad_rows(vn_ref[...], page).astype(BF16)], t_k < t_q)

    def heads(buf, i):
        return jnp.concatenate([buf[slot, i, pl.ds(kv, page, stride=SB_KV_HEADS), :] for kv in range(SB_KV_HEADS)],
                               axis=1).astype(BF16)

    order = range(pages - 1, -1, -1)
    attend([heads(k_buf, i) for i in order], [heads(v_buf, i) for i in order], None)

    @pl.when(step == steps - 1)
    def _():
        acc = acc_scr[...]
        o_ref[0] = jnp.concatenate([acc[kv * half:(kv + 1) * half, kv * SB_HEAD_DIM:(kv + 1) * SB_HEAD_DIM]
                                    for kv in range(SB_KV_HEADS)], axis=0).astype(BF16)

    @pl.when(g == last)
    def _():
        for extra in range(1, PAGE_RING):
            for c in page_copies(None, lax.rem(g + extra, PAGE_RING)):
                c.wait()


def _sb_paged(page_table, q, k_new, v_new, cache_k, cache_v, upper, pages=16):
    batch, rows, _ = q.shape
    t_new = rows // SB_HEADS
    n_pages = page_table.shape[1]
    page = cache_k.shape[1] // SB_KV_HEADS
    steps = n_pages // pages
    kvw = SB_KV_HEADS * SB_HEAD_DIM
    pt = page_table.reshape(-1)

    assert n_pages % pages == 0
    page_rows = page * SB_KV_HEADS
    grid_spec = pltpu.PrefetchScalarGridSpec(
        num_scalar_prefetch=1,
        grid=(batch, steps),
        in_specs=[pl.BlockSpec((1, rows, SB_HEAD_DIM), lambda b, s, pt_ref: (b, 0, 0)),
                  pl.BlockSpec((t_new, kvw), lambda b, s, pt_ref: (b, 0)),
                  pl.BlockSpec((t_new, kvw), lambda b, s, pt_ref: (b, 0)),
                  pl.BlockSpec((page, page), lambda b, s, pt_ref: (0, 0)),
                  pl.BlockSpec(memory_space=pl.ANY),
                  pl.BlockSpec(memory_space=pl.ANY)],
        out_specs=pl.BlockSpec((1, rows, SB_HEAD_DIM), lambda b, s, pt_ref: (b, 0, 0)),
        scratch_shapes=[pltpu.VMEM((PAGE_RING, pages, page_rows, SB_HEAD_DIM), F32),
                        pltpu.VMEM((PAGE_RING, pages, page_rows, SB_HEAD_DIM), F32),
                        pltpu.SemaphoreType.DMA((PAGE_RING, 2)),
                        pltpu.VMEM((rows, 1), F32), pltpu.VMEM((rows, kvw), F32)],
    )
    return pl.pallas_call(
        functools.partial(_sb_paged_kernel, pages=pages, page=page, t_new=t_new),
        grid_spec=grid_spec,
        out_shape=jax.ShapeDtypeStruct((batch, rows, SB_HEAD_DIM), BF16),
        compiler_params=_cparams(("arbitrary", "arbitrary")),
        name="sb_paged",
    )(pt, q, k_new, v_new, upper, cache_k, cache_v)


def _merge_kernel(oa_ref, ob_ref, ga_ref, gb_ref, wpa_ref, wpb_ref, o_ref):
    a = _dot(oa_ref[...], wpa_ref[...])
    b = _dot(ob_ref[...], wpb_ref[...])
    o_ref[...] = (_sigmoid(ga_ref[...]) * a + _sigmoid(gb_ref[...]) * b).astype(BF16)


def _merge(o_a, o_b, proj, wpa, wpb, tm=512, tn=1024):
    n, inner = o_a.shape
    d = wpa.shape[1]
    return pl.pallas_call(
        _merge_kernel,
        grid=(n // tm, d // tn),
        in_specs=[pl.BlockSpec((tm, inner), lambda i, j: (i, 0)),
                  pl.BlockSpec((tm, inner), lambda i, j: (i, 0)),
                  pl.BlockSpec((tm, tn), lambda i, j: (i, COL_GATE_A // tn + j)),
                  pl.BlockSpec((tm, tn), lambda i, j: (i, COL_GATE_B // tn + j)),
                  pl.BlockSpec((inner, tn), lambda i, j: (0, j)),
                  pl.BlockSpec((inner, tn), lambda i, j: (0, j))],
        out_specs=pl.BlockSpec((tm, tn), lambda i, j: (i, j)),
        out_shape=jax.ShapeDtypeStruct((n, d), BF16),
        compiler_params=_cparams(("parallel", "arbitrary")),
        name="gated_merge",
    )(o_a, o_b, proj, proj, wpa, wpb)


def _outproj_kernel(m_ref, x_ref, wo_ref, g_ref, wr_ref, br_ref, x1_ref, h_ref, lg_ref):
    x1 = x_ref[...] + _dot(m_ref[...], wo_ref[...])
    x1_ref[...] = x1
    h = _rms(x1, g_ref[...])
    h_ref[...] = h
    h_hi = h.astype(BF16)
    h_lo = (h - h_hi.astype(F32)).astype(BF16)
    wr = wr_ref[...]
    w_hi = wr.astype(BF16)
    w_lo = (wr - w_hi.astype(F32)).astype(BF16)
    lg_ref[...] = _dot(h_hi, w_hi) + _dot(h_lo, w_hi) + _dot(h_hi, w_lo) + br_ref[...]


def _outproj(merged, x, wo, g_ffn, w_router, b_router, tm=256):
    n, d = x.shape
    rw = w_router.shape[1]
    return pl.pallas_call(
        _outproj_kernel,
        grid=(n // tm,),
        in_specs=[pl.BlockSpec((tm, d), lambda i: (i, 0)),
                  pl.BlockSpec((tm, d), lambda i: (i, 0)),
                  pl.BlockSpec((d, d), lambda i: (0, 0)),
                  pl.BlockSpec((1, d), lambda i: (0, 0)),
                  pl.BlockSpec((d, rw), lambda i: (0, 0)),
                  pl.BlockSpec((1, rw), lambda i: (0, 0))],
        out_specs=[pl.BlockSpec((tm, d), lambda i: (i, 0)),
                   pl.BlockSpec((tm, d), lambda i: (i, 0)),
                   pl.BlockSpec((tm, rw), lambda i: (i, 0))],
        out_shape=[jax.ShapeDtypeStruct((n, d), F32), jax.ShapeDtypeStruct((n, d), F32),
                   jax.ShapeDtypeStruct((n, rw), F32)],
        compiler_params=_cparams(("parallel",)),
        name="out_proj_router",
    )(merged, x, wo, g_ffn, w_router, b_router)


def _route_kernel(lg_ref, id_ref, w_ref):
    lg = lg_ref[...]
    col = lax.broadcasted_iota(jnp.int32, lg.shape, 1)
    big = jnp.int32(1 << 20)

    def first_argmax(v):
        m = jnp.max(v, axis=-1, keepdims=True)
        return m, jnp.min(jnp.where(v == m, col, big), axis=-1, keepdims=True)

    gl = jnp.where(col < N_GROUPS, lg, -jnp.inf)
    g_max, g_idx = first_argmax(gl)
    g_w = 1.0 / jnp.sum(jnp.exp(gl - g_max), axis=-1, keepdims=True)
    lo = N_GROUPS + g_idx * EXPERTS_PER_GROUP
    el = jnp.where(jnp.logical_and(col >= lo, col < lo + EXPERTS_PER_GROUP), lg, -jnp.inf)
    v1, i1 = first_argmax(el)
    v2, i2 = first_argmax(jnp.where(col == i1, -jnp.inf, el))
    e2 = jnp.exp(v2 - v1)
    w1 = 1.0 / (1.0 + e2) * g_w
    w2 = e2 / (1.0 + e2) * g_w
    id_ref[...] = jnp.where(col == 0, i1 - N_GROUPS, jnp.where(col == 1, i2 - N_GROUPS, 0))
    w_ref[...] = jnp.where(col == 0, w1, jnp.where(col == 1, w2, 0.0))


def _route(logits, tm=512):
    n, rw = logits.shape
    spec = pl.BlockSpec((tm, rw), lambda i: (i, 0))
    return pl.pallas_call(
        _route_kernel,
        grid=(n // tm,),
        in_specs=[spec],
        out_specs=[spec, spec],
        out_shape=[jax.ShapeDtypeStruct((n, rw), jnp.int32), jax.ShapeDtypeStruct((n, rw), F32)],
        compiler_params=_cparams(("parallel",)),
        name="route",
    )(logits)


def _moe_kernel(te_ref, nu_ref, src_ref, dst_ref, rw_ref, wg_ref, wu_ref, wd_ref, h_hbm, y_hbm,
                x_buf, o_buf, sem_in, sem_out, *, tm):
    tile = pl.program_id(0)
    last = pl.num_programs(0) - 1
    slot = lax.rem(tile, 2)

    def row_reads(tile_idx, slot_idx):
        out = []
        for r in range(tm):
            src = 0 if tile_idx is None else src_ref[tile_idx * tm + r]
            out.append(pltpu.make_async_copy(h_hbm.at[pl.ds(src, 1)], x_buf.at[slot_idx, pl.ds(r, 1)],
                                             sem_in.at[slot_idx]))
        return out

    def row_writes(tile_idx, slot_idx):
        out = []
        for r in range(tm):
            dst = 0 if tile_idx is None else dst_ref[tile_idx * tm + r]
            out.append(pltpu.make_async_copy(o_buf.at[slot_idx, pl.ds(r, 1)], y_hbm.at[pl.ds(dst, 1)],
                                             sem_out.at[slot_idx]))
        return out

    @pl.when(tile == 0)
    def _():
        for r, c in enumerate(row_reads(0, 0)):
            c.start(priority=r % 2)

    for c in row_reads(None, slot):
        c.wait()
    for r, c in enumerate(row_reads(jnp.minimum(tile + 1, last), 1 - slot)):
        c.start(priority=r % 2)

    @pl.when(tile < nu_ref[0])
    def _():
        x = x_buf[slot].astype(BF16)
        g = _dot(x, wg_ref[0].astype(BF16))
        u = _dot(x, wu_ref[0].astype(BF16))
        hid = (g * _sigmoid(g)) * u * rw_ref[...]
        o_buf[slot] = _dot(hid.astype(BF16), wd_ref[0].astype(BF16))

    @pl.when(tile >= nu_ref[0])
    def _():
        o_buf[slot] = jnp.zeros(o_buf.shape[1:], F32)

    @pl.when(tile > 0)
    def _():
        for c in row_writes(None, 1 - slot):
            c.wait()

    for r, c in enumerate(row_writes(tile, slot)):
        c.start(priority=r % 2)

    @pl.when(tile == last)
    def _():
        for c in row_writes(None, slot):
            c.wait()
        for c in row_reads(None, 1 - slot):
            c.wait()


def _moe(tile_expert, n_used, src_tok, dst_row, h, row_w, wg, wu, wd, out_rows, tm):
    rows = src_tok.shape[0]
    d = h.shape[1]
    ff = wg.shape[2]
    grid_spec = pltpu.PrefetchScalarGridSpec(
        num_scalar_prefetch=4,
        grid=(rows // tm,),
        in_specs=[pl.BlockSpec((tm, 1), lambda t, te, nu, src, dst: (t, 0)),
                  pl.BlockSpec((1, d, ff), lambda t, te, nu, src, dst: (te[t], 0, 0)),
                  pl.BlockSpec((1, d, ff), lambda t, te, nu, src, dst: (te[t], 0, 0)),
                  pl.BlockSpec((1, ff, d), lambda t, te, nu, src, dst: (te[t], 0, 0)),
                  pl.BlockSpec(memory_space=pl.ANY)],
        out_specs=pl.BlockSpec(memory_space=pl.ANY),
        scratch_shapes=[pltpu.VMEM((2, tm, d), F32), pltpu.VMEM((2, tm, d), F32),
                        pltpu.SemaphoreType.DMA((2,)), pltpu.SemaphoreType.DMA((2,))],
    )
    return pl.pallas_call(
        functools.partial(_moe_kernel, tm=tm),
        grid_spec=grid_spec,
        out_shape=jax.ShapeDtypeStruct((out_rows, d), F32),
        compiler_params=_cparams(("arbitrary",)),
        name="moe_experts",
    )(tile_expert, n_used, src_tok, dst_row, row_w, wg, wu, wd, h)


def _expert_rows(h, ids, wts, wg, wu, wd, tm=256):
    n, d = h.shape
    n_assign = n * TOP_K
    n_tiles = n_assign // tm + N_EXPERTS
    eid = ids[:, :TOP_K].reshape(n_assign)
    gate = wts[:, :TOP_K].reshape(n_assign)
    a_idx = jnp.arange(n_assign, dtype=jnp.int32)
    eid_sorted, a_sorted, gate_sorted = lax.sort((eid, a_idx, gate), num_keys=1, is_stable=True)
    bounds = jnp.searchsorted(eid_sorted, jnp.arange(N_EXPERTS + 1, dtype=jnp.int32), side="left").astype(jnp.int32)
    counts = bounds[1:] - bounds[:-1]
    padded = ((counts + tm - 1) // tm) * tm
    pad_end = jnp.cumsum(padded)
    pad_start = pad_end - padded
    start = jnp.cumsum(counts) - counts
    n_used = (pad_end[-1] // tm).astype(jnp.int32)
    tile_start = jnp.minimum(jnp.arange(n_tiles, dtype=jnp.int32), n_used - 1) * tm
    tile_expert = jnp.searchsorted(pad_end, tile_start, side="right").astype(jnp.int32)
    row = jnp.arange(n_tiles * tm, dtype=jnp.int32)
    per_row = lambda per_tile: jnp.broadcast_to(per_tile[:, None], (n_tiles, tm)).reshape(n_tiles * tm)
    rank = row - per_row(pad_start[tile_expert])
    valid = jnp.logical_and(rank < per_row(counts[tile_expert]), row < n_used * tm)
    src = jnp.clip(per_row(start[tile_expert]) + rank, 0, n_assign - 1)
    assign = a_sorted[src]
    src_tok = jnp.where(valid, assign // TOP_K, 0)
    dst_row = jnp.where(valid, (assign % TOP_K) * n + assign // TOP_K, n_assign + row % tm)
    row_w = jnp.where(valid, gate_sorted[src], 0.0)
    return _moe(tile_expert, n_used[None], src_tok, dst_row, h, row_w[:, None], wg, wu, wd, n_assign + tm, tm)


def _final_kernel(x_ref, y0_ref, y1_ref, g_ref, o_ref):
    o_ref[...] = _rms(x_ref[...] + (y0_ref[...] + y1_ref[...]), g_ref[...])


def _final(x1, y_rows, gain, tm=512):
    n, d = x1.shape
    spec = pl.BlockSpec((tm, d), lambda i: (i, 0))
    return pl.pallas_call(
        _final_kernel,
        grid=(n // tm,),
        in_specs=[spec, spec, pl.BlockSpec((tm, d), lambda i: (i + n // tm, 0)), pl.BlockSpec((1, d), lambda i: (0, 0))],
        out_specs=spec,
        out_shape=jax.ShapeDtypeStruct((n, d), F32),
        compiler_params=_cparams(("parallel",)),
        name="final_norm",
    )(x1, y_rows, y_rows, gain)


def _rotate_half_cols(w):
    half = QK_ROPE // 2
    return jnp.concatenate([-w[..., half:], w[..., :half]], axis=-1)


def _prep_w_in(w_in):
    d = w_in.shape[0]
    offs = np.cumsum([0, Q_LORA, KV_LORA, QK_ROPE, SB_HEADS * SB_HEAD_DIM, SB_KV_HEADS * SB_HEAD_DIM,
                      SB_KV_HEADS * SB_HEAD_DIM, d, d])
    c_q, c_kv, k_pe, sb_q, sb_k, sb_v, g_a, g_b = [w_in[:, offs[i]:offs[i + 1]] for i in range(8)]
    parts = [g_a, g_b, sb_q, c_q, c_kv, sb_k, sb_v, k_pe, _rotate_half_cols(k_pe)]
    used = sum(p.shape[1] for p in parts)
    parts.append(jnp.zeros((d, PROJ_WIDTH - used), w_in.dtype))
    return jnp.concatenate(parts, axis=1).astype(BF16)


def _prep_w_uq(w_uq):
    nope = w_uq[:, :, :QK_NOPE].reshape(Q_LORA, MLA_HEADS * QK_NOPE)
    pe = w_uq[:, :, QK_NOPE:]
    pe_rot = _rotate_half_cols(pe)
    return jnp.concatenate([nope, pe.reshape(Q_LORA, -1), pe_rot.reshape(Q_LORA, -1)], axis=1).astype(BF16)


def _rope_tables(positions):
    half = QK_ROPE // 2
    inv = 1.0 / (ROPE_THETA ** (jnp.arange(half, dtype=F32) / half))
    ang = positions.astype(F32)[:, None] * inv[None, :]
    cos, sin = jnp.cos(ang), jnp.sin(ang)
    cos2 = jnp.concatenate([cos, cos], axis=-1)
    sin2 = jnp.concatenate([sin, sin], axis=-1)
    return cos2, sin2


def _strict_upper(n):
    r = np.arange(n)
    return jnp.asarray((r[:, None] > r[None, :]).astype(np.float32), dtype=BF16)


def kernel(x_prompt, x_sample, cache_ckv, cache_kpe, cache_sb_k, cache_sb_v, page_table, norm_mix, w_in, q_norm,
           w_uq, kv_norm, w_uk, w_uv, w_pa, w_pb, w_o, norm_ffn, w_rg, b_rg, w_re, b_re, w_gate, w_up, w_down,
           norm_final):
    depth = w_in.shape[0]
    assert depth == 1
    batch, seq, d = x_prompt.shape
    dec_batch, dec_seq, _ = x_sample.shape
    n_prompt = batch * seq
    n_sample = dec_batch * dec_seq
    n = n_prompt + n_sample
    page = cache_ckv.shape[2]
    past_len = page_table.shape[1] * page
    post_tm = 256
    l = 0

    pos = jnp.concatenate([jnp.arange(seq), past_len + (jnp.arange(post_tm) % dec_seq)])
    cos2, sin2 = _rope_tables(pos)
    cosq = jnp.tile(cos2, (1, MLA_HEADS))
    sinq = jnp.tile(sin2, (1, MLA_HEADS))
    csk = jnp.concatenate([cos2, sin2], axis=1)
    w1 = _prep_w_in(w_in[l])
    wq = _prep_w_uq(w_uq[l])
    wuk_t = jnp.transpose(w_uk[l], (1, 2, 0)).astype(BF16)
    wuv = jnp.transpose(w_uv[l], (1, 0, 2)).astype(BF16)
    rw = 128
    w_router = jnp.concatenate([w_rg[l], w_re[l], jnp.zeros((d, rw - N_GROUPS - N_EXPERTS), F32)], axis=1)
    b_router = jnp.concatenate([b_rg[l], b_re[l], jnp.zeros((rw - N_GROUPS - N_EXPERTS,), F32)])[None, :]

    x = jnp.concatenate([x_prompt.reshape(n_prompt, d), x_sample.reshape(n_sample, d)], axis=0)
    proj = _inproj(x, norm_mix[l][None, :], w1)
    (ckv, kpe, sbk, sbv, ckvb, kpeb, sbkb, sbvb, qlat, qpe, sbq) = _post(
        proj, q_norm[l][None, :], kv_norm[l][None, :], wq, wuk_t, cosq, sinq, csk, n_prompt, seq, tm=post_tm)

    sb_t = 256
    oa_p = _mla_prompt(qlat, qpe, ckvb, kpeb, wuv, batch, seq)
    ob_p = _sb_prompt(sbq, sbkb, sbvb, _strict_upper(sb_t), batch, seq, t=sb_t)

    def per_seq(a):
        a = a[:, n_prompt:].reshape(a.shape[0], dec_batch, dec_seq, a.shape[2])
        return jnp.transpose(a, (1, 0, 2, 3)).reshape(dec_batch, a.shape[0] * dec_seq, a.shape[3])

    n_pool = cache_ckv.shape[1]
    kvw = SB_KV_HEADS * SB_HEAD_DIM
    o_lat_s = _mla_paged(page_table, per_seq(qlat), per_seq(qpe), ckv[n_prompt:], kpe[n_prompt:],
                         cache_ckv[l], jnp.swapaxes(cache_kpe[l], 1, 2))
    o_lat_s = jnp.transpose(o_lat_s.reshape(dec_batch, MLA_HEADS, dec_seq, KV_LORA), (1, 0, 2, 3))
    oa_s = _uv(o_lat_s.reshape(MLA_HEADS, n_sample, KV_LORA), wuv)
    ob_s = _sb_paged(page_table, per_seq(sbq), sbk[n_prompt:], sbv[n_prompt:],
                     cache_sb_k[l].reshape(n_pool, page * SB_KV_HEADS, SB_HEAD_DIM),
                     cache_sb_v[l].reshape(n_pool, page * SB_KV_HEADS, SB_HEAD_DIM),
                     _strict_upper(page))
    ob_s = jnp.transpose(ob_s.reshape(dec_batch, SB_HEADS, dec_seq, SB_HEAD_DIM), (0, 2, 1, 3))
    ob_s = ob_s.reshape(n_sample, SB_HEADS * SB_HEAD_DIM)

    o_a = jnp.concatenate([oa_p, oa_s], axis=0)
    o_b = jnp.concatenate([ob_p, ob_s], axis=0)
    merged = _merge(o_a, o_b, proj, w_pa[l].astype(BF16), w_pb[l].astype(BF16))
    x1, h2, logits = _outproj(merged, x, w_o[l].astype(BF16), norm_ffn[l][None, :], w_router, b_router)
    ids, wts = _route(logits)

    y_rows = _expert_rows(h2, ids, wts, w_gate[l], w_up[l], w_down[l])
    y = _final(x1, y_rows, norm_final[None, :])

    def rows_p(a, *tail):
        return a[:n_prompt].reshape((depth, batch, seq) + tail)

    def rows_s(a, *tail):
        return a[n_prompt:].reshape((depth, dec_batch, dec_seq) + tail)

    return (y[:n_prompt].reshape(batch, seq, d), y[n_prompt:].reshape(dec_batch, dec_seq, d),
            rows_p(ckv, KV_LORA), rows_p(kpe, QK_ROPE),
            rows_p(sbk, SB_KV_HEADS, SB_HEAD_DIM), rows_p(sbv, SB_KV_HEADS, SB_HEAD_DIM),
            rows_s(ckv, KV_LORA), rows_s(kpe, QK_ROPE),
            rows_s(sbk, SB_KV_HEADS, SB_HEAD_DIM), rows_s(sbv, SB_KV_HEADS, SB_HEAD_DIM))
```
